```python
import math
import jax, jax.numpy as jnp
from jax import lax
import numpy as np


D_MODEL = 1024
BATCH = 2
SEQ = 16384
DEPTH = 2

HEAD_DIM = 64
MLSTM_WIDTH = D_MODEL // 4
MLSTM_HEADS = MLSTM_WIDTH // HEAD_DIM
MLSTM_CHUNK = 64
CONV_WIDTH = 4
S5_WIDTH = D_MODEL // 4
S5_GROUP_CH = 16
S5_GROUPS = S5_WIDTH // S5_GROUP_CH
S5_STATE = 64
S5_DT_MIN = 1e-3
S5_DT_MAX = 1e-1
ATTN_WIDTH = D_MODEL - MLSTM_WIDTH - S5_WIDTH
ATTN_HEADS = ATTN_WIDTH // HEAD_DIM
ATTN_BLOCK = 128
DILATED_PATTERNS = ((128, 1), (512, 4), (2048, 16))
REL_BUCKETS = 32
REL_MAX_EXACT = 16
REL_MAX_DIST = 2048
MEM_TOKENS = 256
MEM_HEADS = 4
MEM_HEAD_DIM = D_MODEL // MEM_HEADS
MOE_GROUPS = 4
EXPERTS_PER_GROUP = 8
N_EXPERTS = MOE_GROUPS * EXPERTS_PER_GROUP
TOP_K = 2
D_EXPERT = D_MODEL // 2
MOE_BLOCK = 128

IN_WIDTH = 2 * MLSTM_WIDTH + 2 * MLSTM_HEADS + S5_WIDTH + 3 * ATTN_WIDTH
RMS_EPS = 1e-6
NEG_INF = -1e30

kernel_name = 'hybrid_mlstm_s5_dilated_hmoe_block'

f32 = jnp.float32


def rms_norm(x, gain):
    xf = x.astype(f32)
    y = xf * lax.rsqrt(jnp.mean(xf * xf, axis=-1, keepdims=True) + RMS_EPS)
    return (y * gain.astype(f32)).astype(x.dtype)


def causal_depthwise_conv(x, w, b):
    k_w, c = w.shape
    y = lax.conv_general_dilated(x, w[:, None, :].astype(x.dtype), window_strides=(1,),
                                 padding=[(k_w - 1, 0)],
                                 dimension_numbers=('NWC', 'WIO', 'NWC'),
                                 feature_group_count=c)
    return y + b.astype(x.dtype)


def mlstm_chunkwise(q, k, v, log_i, log_f):
    B, H, S, Dh = q.shape
    L = MLSTM_CHUNK
    nc = S // L
    q = q.reshape(B, H, nc, L, Dh)
    k = k.reshape(B, H, nc, L, Dh)
    v = v.reshape(B, H, nc, L, Dh)
    log_i = log_i.reshape(B, H, nc, L)
    b = jnp.cumsum(log_f.reshape(B, H, nc, L), axis=-1)
    g = b[..., -1]
    a = g[..., None] - b + log_i

    def step(carry, xs):
        c_st, n_st, m_st = carry
        g_c, a_c, k_c, v_c = xs
        m_new = jnp.maximum(g_c + m_st, jnp.max(a_c, axis=-1))
        decay = jnp.exp(g_c + m_st - m_new)
        w = jnp.exp(a_c - m_new[..., None])
        c_new = decay[..., None, None] * c_st + jnp.einsum('bhl,bhld,bhle->bhde', w, k_c, v_c)
        n_new = decay[..., None] * n_st + jnp.einsum('bhl,bhld->bhd', w, k_c)
        return (c_new, n_new, m_new), (c_st, n_st, m_st)

    init = (jnp.zeros((B, H, Dh, Dh), f32), jnp.zeros((B, H, Dh), f32), jnp.zeros((B, H), f32))
    xs = (jnp.moveaxis(g, 2, 0), jnp.moveaxis(a, 2, 0), jnp.moveaxis(k, 2, 0), jnp.moveaxis(v, 2, 0))
    _, (c_prev, n_prev, m_prev) = lax.scan(step, init, xs)
    c_prev = jnp.moveaxis(c_prev, 0, 2)
    n_prev = jnp.moveaxis(n_prev, 0, 2)
    m_prev = jnp.moveaxis(m_prev, 0, 2)

    causal = np.tril(np.ones((L, L), dtype=bool))
    d_mat = jnp.where(causal, b[..., :, None] - b[..., None, :] + log_i[..., None, :], -jnp.inf)
    m_inter = b + m_prev[..., None]
    m_t = jnp.maximum(m_inter, jnp.max(d_mat, axis=-1))
    w_intra = jnp.exp(d_mat - m_t[..., None]) * jnp.einsum('bhcld,bhcsd->bhcls', q, k)
    inter = jnp.exp(m_inter - m_t)
    num = inter[..., None] * jnp.einsum('bhcld,bhcde->bhcle', q, c_prev) \
        + jnp.einsum('bhcls,bhcse->bhcle', w_intra, v)
    den = inter * jnp.einsum('bhcld,bhcd->bhcl', q, n_prev) + jnp.sum(w_intra, axis=-1)
    h = num / jnp.maximum(jnp.abs(den), jnp.exp(-m_t))[..., None]
    return h.reshape(B, H, S, Dh)


def mlstm_mixer(u, o_pre, i_pre, f_pre, conv_w, conv_b, wq, wk, wv, gate_bias, norm_gain, skip):
    B, S, W = u.shape
    H, Dh = MLSTM_HEADS, HEAD_DIM
    c = jax.nn.silu(causal_depthwise_conv(u, conv_w, conv_b))
    ch = c.reshape(B, S, H, Dh).astype(f32)
    uh = u.reshape(B, S, H, Dh).astype(f32)
    q = jnp.einsum('bshd,hde->bhse', ch, wq.astype(f32)) * (Dh ** -0.5)
    k = jnp.einsum('bshd,hde->bhse', ch, wk.astype(f32))
    v = jnp.einsum('bshd,hde->bhse', uh, wv.astype(f32))
    gb = gate_bias.astype(f32)
    log_i = jnp.transpose(i_pre.astype(f32) + gb[:H], (0, 2, 1))
    log_f = jnp.transpose(jax.nn.log_sigmoid(f_pre.astype(f32) + gb[H:]), (0, 2, 1))
    h = jnp.transpose(mlstm_chunkwise(q, k, v, log_i, log_f), (0, 2, 1, 3))
    h = h * lax.rsqrt(jnp.mean(h * h, axis=-1, keepdims=True) + RMS_EPS)
    h = h.reshape(B, S, W) * norm_gain.astype(f32) + skip.astype(f32) * c.astype(f32)
    return (jax.nn.sigmoid(o_pre.astype(f32)) * h).astype(u.dtype)


def s5_mixer(u, a_re, a_im, log_dt, b_re, b_im, c_re, c_im, d_skip, w_glu, b_glu):
    B, S, W = u.shape
    G, Hc, P = S5_GROUPS, S5_GROUP_CH, S5_STATE
    uf = u.astype(f32).reshape(B, S, G, Hc)
    a_re = a_re.astype(f32)
    a_im = a_im.astype(f32)
    dt = jnp.exp(log_dt.astype(f32))[:, None]
    mag = jnp.exp(a_re * dt)
    ab_re = mag * jnp.cos(a_im * dt)
    ab_im = mag * jnp.sin(a_im * dt)
    den = a_re * a_re + a_im * a_im
    xr, yi = ab_re - 1.0, ab_im
    coef_re = (xr * a_re + yi * a_im) / den
    coef_im = (yi * a_re - xr * a_im) / den
    b_re = b_re.astype(f32)
    b_im = b_im.astype(f32)
    bb_re = coef_re[..., None] * b_re - coef_im[..., None] * b_im
    bb_im = coef_re[..., None] * b_im + coef_im[..., None] * b_re
    bu_re = jnp.einsum('bsgh,gph->bsgp', uf, bb_re)
    bu_im = jnp.einsum('bsgh,gph->bsgp', uf, bb_im)
    a_seq_re = jnp.broadcast_to(ab_re, (1, S, G, P))
    a_seq_im = jnp.broadcast_to(ab_im, (1, S, G, P))

    def combine(e1, e2):
        a1r, a1i, b1r, b1i = e1
        a2r, a2i, b2r, b2i = e2
        return (a2r * a1r - a2i * a1i, a2r * a1i + a2i * a1r,
                a2r * b1r - a2i * b1i + b2r, a2r * b1i + a2i * b1r + b2i)

    _, _, st_re, st_im = lax.associative_scan(combine, (a_seq_re, a_seq_im, bu_re, bu_im), axis=1)
    y = jnp.einsum('bsgp,ghp->bsgh', st_re, c_re.astype(f32)) \
        - jnp.einsum('bsgp,ghp->bsgh', st_im, c_im.astype(f32)) \
        + d_skip.astype(f32) * uf
    z = jax.nn.gelu(y.reshape(B, S, W))
    out = z * jax.nn.sigmoid(z @ w_glu.astype(f32) + b_glu.astype(f32))
    return out.astype(u.dtype)


def t5_bucket(dist):
    exact = dist < REL_MAX_EXACT
    large = REL_MAX_EXACT + (np.log(np.maximum(dist, 1) / REL_MAX_EXACT)
                             / np.log(REL_MAX_DIST / REL_MAX_EXACT)
                             * (REL_BUCKETS - REL_MAX_EXACT)).astype(np.int32)
    large = np.minimum(large, REL_BUCKETS - 1)
    return np.where(exact, dist, large).astype(np.int32)


def dilated_bias_mask(window, dil, n_blocks, rel_table):
    n_back = window // dil
    i = np.arange(ATTN_BLOCK)[:, None]
    j = np.arange(2 * ATTN_BLOCK)[None, :]
    off = ATTN_BLOCK + i - j
    band = (off >= 0) & (off <= n_back)
    bucket = t5_bucket(np.clip(off, 0, None) * dil)
    bias = jnp.transpose(rel_table.astype(f32)[bucket], (2, 0, 1))
    key_valid = (np.arange(n_blocks)[:, None] > 0) | (j >= ATTN_BLOCK)
    mask = band[None, None, None, None] & key_valid[None, :, None, None, None, :]
    return bias, mask


def prev_block(x):
    return jnp.concatenate([jnp.zeros_like(x[:, :1]), x[:, :-1]], axis=1)


def dilated_attention(q, k, v, rel_table):
    B, S, H, Dh = q.shape
    q = q.astype(f32) * (Dh ** -0.5)
    k = k.astype(f32)
    v = v.astype(f32)
    outs, lses = [], []
    for window, dil in DILATED_PATTERNS:
        span = ATTN_BLOCK * dil
        s_pad = -(-S // span) * span
        nb = s_pad // span
        pad_w = ((0, 0), (0, s_pad - S), (0, 0), (0, 0))
        qb = jnp.pad(q, pad_w).reshape(B, nb, ATTN_BLOCK, dil, H, Dh)
        kb = jnp.pad(k, pad_w).reshape(B, nb, ATTN_BLOCK, dil, H, Dh)
        vb = jnp.pad(v, pad_w).reshape(B, nb, ATTN_BLOCK, dil, H, Dh)
        kw = jnp.concatenate([prev_block(kb), kb], axis=2)
        vw = jnp.concatenate([prev_block(vb), vb], axis=2)
        bias, mask = dilated_bias_mask(window, dil, nb, rel_table)
        s = jnp.einsum('bnirhe,bnjrhe->bnrhij', qb, kw) + bias
        s = jnp.where(mask, s, NEG_INF)
        m = jnp.max(s, axis=-1, keepdims=True)
        p = jnp.exp(s - m)
        l = jnp.sum(p, axis=-1, keepdims=True)
        o = jnp.einsum('bnrhij,bnjrhe->bnirhe', p / l, vw)
        lse = jnp.transpose((m + jnp.log(l))[..., 0], (0, 1, 4, 2, 3))
        outs.append(o.reshape(B, s_pad, H, Dh)[:, :S])
        lses.append(lse.reshape(B, s_pad, H)[:, :S])
    w = jax.nn.softmax(jnp.stack(lses), axis=0)
    return jnp.sum(w[..., None] * jnp.stack(outs), axis=0)


def memory_cross_attention(xn, memn, w_q, w_kv, w_o):
    B, S, D = xn.shape
    M = memn.shape[1]
    q = (xn @ w_q).reshape(B, S, MEM_HEADS, MEM_HEAD_DIM)
    k, v = jnp.split(memn @ w_kv, 2, axis=-1)
    k = k.reshape(B, M, MEM_HEADS, MEM_HEAD_DIM)
    v = v.reshape(B, M, MEM_HEADS, MEM_HEAD_DIM)
    s = jnp.einsum('bshe,bmhe->bhsm', q, k).astype(f32) * (MEM_HEAD_DIM ** -0.5)
    p = jax.nn.softmax(s, axis=-1).astype(v.dtype)
    o = jnp.einsum('bhsm,bmhe->bshe', p, v).reshape(B, S, D)
    return o @ w_o


def hier_moe(xn, w_group, b_group, w_expert, b_expert, w_gu, w_down):
    B, S, D = xn.shape
    T = B * S
    xt = xn.reshape(T, D)
    g_prob = jax.nn.softmax((xt @ w_group).astype(f32) + b_group.astype(f32), axis=-1)
    g_p, g_i = lax.top_k(g_prob, 1)
    e_logits = ((xt @ w_expert).astype(f32) + b_expert.astype(f32)).reshape(T, MOE_GROUPS, EXPERTS_PER_GROUP)
    e_logits = jnp.take_along_axis(e_logits, g_i[:, :, None], axis=1)[:, 0]
    e_p, e_i = lax.top_k(jax.nn.softmax(e_logits, axis=-1), TOP_K)
    gates = g_p * e_p / jnp.sum(e_p, axis=-1, keepdims=True)
    eid = (g_i * EXPERTS_PER_GROUP + e_i).reshape(-1)
    gate = gates.reshape(-1).astype(xt.dtype)
    tok = jnp.repeat(jnp.arange(T, dtype=jnp.int32), TOP_K)
    n_assign = T * TOP_K
    order = jnp.argsort(eid)
    eid_s, tok_s, gate_s = eid[order], tok[order], gate[order]
    counts = jnp.bincount(eid, length=N_EXPERTS)
    starts = jnp.cumsum(counts) - counts
    padded = (counts + MOE_BLOCK - 1) // MOE_BLOCK * MOE_BLOCK
    p_ends = jnp.cumsum(padded)
    p_starts = p_ends - padded
    dest = p_starts[eid_s] + jnp.arange(n_assign, dtype=jnp.int32) - starts[eid_s]
    n_blocks = -(-n_assign // MOE_BLOCK) + N_EXPERTS
    n_rows = n_blocks * MOE_BLOCK
    row_tok = jnp.full((n_rows,), T, jnp.int32).at[dest].set(tok_s)
    row_gate = jnp.zeros((n_rows,), xt.dtype).at[dest].set(gate_s)
    block_e = jnp.minimum(jnp.searchsorted(p_ends, jnp.arange(n_blocks) * MOE_BLOCK, side='right'),
                          N_EXPERTS - 1)
    x_rows = jnp.concatenate([xt, jnp.zeros((1, D), xt.dtype)], axis=0)[row_tok]
    x_rows = x_rows.reshape(n_blocks, MOE_BLOCK, D)

    def expert_block(args):
        xb, e = args
        a, b = jnp.split(xb @ w_gu[e], 2, axis=-1)
        return (jax.nn.silu(a) * b) @ w_down[e]

    y_rows = lax.map(expert_block, (x_rows, block_e)).reshape(n_rows, D)
    out = jnp.zeros((T + 1, D), xt.dtype).at[row_tok].add(y_rows * row_gate[:, None])[:T]
    return out.reshape(B, S, D)


def setup_inputs(seed: int = 0) -> dict:
    key = jax.random.key(seed)
    ks = iter(jax.random.split(key, 48))

    def nrm(shape, scale):
        return jax.random.normal(next(ks), shape, f32) * scale

    def gain(shape):
        return 1.0 + nrm(shape, 0.02)

    L, D = DEPTH, D_MODEL
    H, G, P, Hc = MLSTM_HEADS, S5_GROUPS, S5_STATE, S5_GROUP_CH
    n_idx = jnp.arange(P, dtype=f32)
    gate_bias = jnp.concatenate([nrm((L, H), 0.1),
                                 jnp.linspace(3.0, 6.0, H, dtype=f32)[None] + nrm((L, H), 0.1)], axis=-1)
    return {
        'x': nrm((BATCH, SEQ, D), 1.0),
        'mem': nrm((BATCH, MEM_TOKENS, D), 1.0),
        'rel_bias': nrm((REL_BUCKETS, ATTN_HEADS), 0.5),
        'norm_mix': gain((L, D)),
        'w_in': nrm((L, D, IN_WIDTH), D ** -0.5),
        'mlstm_conv_w': nrm((L, CONV_WIDTH, MLSTM_WIDTH), CONV_WIDTH ** -0.5),
        'mlstm_conv_b': nrm((L, MLSTM_WIDTH), 0.02),
        'mlstm_wq': nrm((L, H, HEAD_DIM, HEAD_DIM), HEAD_DIM ** -0.5),
        'mlstm_wk': nrm((L, H, HEAD_DIM, HEAD_DIM), HEAD_DIM ** -0.5),
        'mlstm_wv': nrm((L, H, HEAD_DIM, HEAD_DIM), HEAD_DIM ** -0.5),
        'mlstm_gate_bias': gate_bias,
        'mlstm_norm': gain((L, MLSTM_WIDTH)),
        'mlstm_skip': gain((L, MLSTM_WIDTH)),
        's5_a_re': -0.5 + nrm((L, G, P), 0.01),
        's5_a_im': math.pi * n_idx[None, None, :] + nrm((L, G, P), 0.01),
        's5_log_dt': jax.random.uniform(next(ks), (L, G), f32, math.log(S5_DT_MIN), math.log(S5_DT_MAX)),
        's5_b_re': nrm((L, G, P, Hc), (2.0 * Hc) ** -0.5),
        's5_b_im': nrm((L, G, P, Hc), (2.0 * Hc) ** -0.5),
        's5_c_re': nrm((L, G, Hc, P), (2.0 * P) ** -0.5),
        's5_c_im': nrm((L, G, Hc, P), (2.0 * P) ** -0.5),
        's5_d': nrm((L, G, Hc), 1.0),
        's5_w_glu': nrm((L, S5_WIDTH, S5_WIDTH), S5_WIDTH ** -0.5),
        's5_b_glu': nrm((L, S5_WIDTH), 0.02),
        's5_out_norm': gain((L, S5_WIDTH)),
        'attn_out_norm': gain((L, ATTN_WIDTH)),
        'w_out': nrm((L, D, D), D ** -0.5),
        'norm_xattn': gain((L, D)),
        'norm_mem': gain((L, D)),
        'xattn_w_q': nrm((L, D, D), D ** -0.5),
        'xattn_w_kv': nrm((L, D, 2 * D), D ** -0.5),
        'xattn_w_o': nrm((L, D, D), D ** -0.5),
        'norm_ffn': gain((L, D)),
        'router_w_group': nrm((L, D, MOE_GROUPS), D ** -0.5),
        'router_b_group': nrm((L, MOE_GROUPS), 0.01),
        'router_w_expert': nrm((L, D, N_EXPERTS), D ** -0.5),
        'router_b_expert': nrm((L, N_EXPERTS), 0.01),
        'expert_w_gu': nrm((L, N_EXPERTS, D, 2 * D_EXPERT), D ** -0.5),
        'expert_w_down': nrm((L, N_EXPERTS, D_EXPERT, D), D_EXPERT ** -0.5),
        'final_norm': gain((D,)),
    }


def reference(x, mem, rel_bias, norm_mix, w_in, mlstm_conv_w, mlstm_conv_b, mlstm_wq, mlstm_wk,
              mlstm_wv, mlstm_gate_bias, mlstm_norm, mlstm_skip, s5_a_re, s5_a_im, s5_log_dt,
              s5_b_re, s5_b_im, s5_c_re, s5_c_im, s5_d, s5_w_glu, s5_b_glu, s5_out_norm,
              attn_out_norm, w_out, norm_xattn, norm_mem, xattn_w_q, xattn_w_kv, xattn_w_o,
              norm_ffn, router_w_group, router_b_group, router_w_expert, router_b_expert,
              expert_w_gu, expert_w_down, final_norm):
    B, S, D = x.shape
    sizes = [MLSTM_WIDTH, MLSTM_WIDTH, MLSTM_HEADS, MLSTM_HEADS, S5_WIDTH,
             ATTN_WIDTH, ATTN_WIDTH, ATTN_WIDTH]
    split_at = [int(s) for s in np.cumsum(sizes)[:-1]]
    for l in range(DEPTH):
        proj = rms_norm(x, norm_mix[l]) @ w_in[l]
        m_u, m_o, m_i, m_f, s_u, a_q, a_k, a_v = jnp.split(proj, split_at, axis=-1)
        y_m = mlstm_mixer(m_u, m_o, m_i, m_f, mlstm_conv_w[l], mlstm_conv_b[l], mlstm_wq[l],
                          mlstm_wk[l], mlstm_wv[l], mlstm_gate_bias[l], mlstm_norm[l], mlstm_skip[l])
        y_s = s5_mixer(s_u, s5_a_re[l], s5_a_im[l], s5_log_dt[l], s5_b_re[l], s5_b_im[l],
                       s5_c_re[l], s5_c_im[l], s5_d[l], s5_w_glu[l], s5_b_glu[l])
        y_s = rms_norm(y_s, s5_out_norm[l])
        y_a = dilated_attention(a_q.reshape(B, S, ATTN_HEADS, HEAD_DIM),
                                a_k.reshape(B, S, ATTN_HEADS, HEAD_DIM),
                                a_v.reshape(B, S, ATTN_HEADS, HEAD_DIM), rel_bias)
        y_a = rms_norm(y_a.reshape(B, S, ATTN_WIDTH).astype(x.dtype), attn_out_norm[l])
        x = x + jnp.concatenate([y_m, y_s, y_a], axis=-1) @ w_out[l]
        x = x + memory_cross_attention(rms_norm(x, norm_xattn[l]), rms_norm(mem, norm_mem[l]),
                                       xattn_w_q[l], xattn_w_kv[l], xattn_w_o[l])
        x = x + hier_moe(rms_norm(x, norm_ffn[l]), router_w_group[l], router_b_group[l],
                         router_w_expert[l], router_b_expert[l], expert_w_gu[l], expert_w_down[l])
    return rms_norm(x, final_norm)
```

```python
import functools
import math

import numpy as np
import jax
import jax.numpy as jnp
from jax import lax
from jax.experimental import pallas as pl
from jax.experimental.pallas import tpu as pltpu

f32 = jnp.float32
bf16 = jnp.bfloat16
i32 = jnp.int32

HEAD_DIM = 64
MLSTM_HEADS = 4
MLSTM_WIDTH = MLSTM_HEADS * HEAD_DIM
CONV_WIDTH = 4
S5_GROUPS = 16
S5_GROUP_CH = 16
S5_STATE = 64
S5_WIDTH = S5_GROUPS * S5_GROUP_CH
S5_LANES = S5_GROUPS * S5_STATE
ATTN_HEADS = 8
ATTN_WIDTH = ATTN_HEADS * HEAD_DIM
ATTN_BLOCK = 128
DILATED_PATTERNS = ((128, 1), (512, 4), (2048, 16))
REL_BUCKETS = 32
REL_MAX_EXACT = 16
REL_MAX_DIST = 2048
MEM_HEADS = 4
MOE_GROUPS = 4
EXPERTS_PER_GROUP = 8
N_EXPERTS = MOE_GROUPS * EXPERTS_PER_GROUP
RMS_EPS = 1e-6
NEG_INF = -1e30

LANES = 128
SUBLANES = 8
VMEM_LIMIT = 48 * 1024 * 1024

MLSTM_PAD = MLSTM_HEADS * LANES
MLSTM_CHUNK = 128
S5_BLOCK = 256
MOE_ROWS = 256
TOKEN_BLOCK = 512


def _cparams(*sem):
    return pltpu.CompilerParams(dimension_semantics=sem, vmem_limit_bytes=VMEM_LIMIT)


def _rms(x, gain):
    return x * lax.rsqrt(jnp.mean(x * x, axis=-1, keepdims=True) + RMS_EPS) * gain


def _dot(a, b):
    return jnp.dot(a, b, preferred_element_type=f32)


def _dot_nt(a, b):
    return lax.dot_general(a, b, (((1,), (1,)), ((), ())), preferred_element_type=f32)


def _dot_tn(a, b):
    return lax.dot_general(a, b, (((0,), (0,)), ((), ())), preferred_element_type=f32)


def _dot_exact(a, b):
    return jnp.dot(a, b, preferred_element_type=f32, precision=lax.Precision.HIGHEST)


def _full(shape):
    n = len(shape)
    return pl.BlockSpec(shape, lambda *_: (0,) * n)


IN_SEGS = (MLSTM_PAD, MLSTM_PAD, LANES, S5_WIDTH, ATTN_WIDTH, ATTN_WIDTH, ATTN_WIDTH)


def _in_proj_kernel(x_ref, g_ref, w_ref, mu_ref, mo_ref, gt_ref, su_ref, q_ref, k_ref, v_ref):
    xn = _rms(x_ref[...], g_ref[...]).astype(bf16)
    outs = (mu_ref, mo_ref, gt_ref, su_ref, q_ref, k_ref, v_ref)
    off = 0
    for o_ref, width in zip(outs, IN_SEGS):
        o_ref[...] = _dot(xn, w_ref[:, off:off + width]).astype(o_ref.dtype)
        off += width


def _in_proj(x2, gain, w_cat):
    t, d = x2.shape
    tm = min(TOKEN_BLOCK, t)
    dts = (f32, f32, f32, f32, bf16, bf16, bf16)
    return pl.pallas_call(
        _in_proj_kernel,
        grid=(t // tm,),
        in_specs=[pl.BlockSpec((tm, d), lambda i: (i, 0)), _full((1, d)), _full(w_cat.shape)],
        out_specs=[pl.BlockSpec((tm, w), lambda i: (i, 0)) for w in IN_SEGS],
        out_shape=[jax.ShapeDtypeStruct((t, w), dt) for w, dt in zip(IN_SEGS, dts)],
        compiler_params=_cparams("parallel"),
        name="in_proj",
    )(x2, gain, w_cat)


def _pad_heads(w, axis):
    shape = w.shape
    h = shape[axis] // HEAD_DIM
    w = w.reshape(shape[:axis] + (h, HEAD_DIM) + shape[axis + 1:])
    pad = [(0, 0)] * w.ndim
    pad[axis + 1] = (0, LANES - HEAD_DIM)
    w = jnp.pad(w, pad)
    return w.reshape(shape[:axis] + (h * LANES,) + shape[axis + 1:])


def _block_diag_heads(w):
    h = w.shape[0]
    wp = jnp.pad(w, ((0, 0), (0, LANES - HEAD_DIM), (0, LANES - HEAD_DIM)))
    eye = jnp.eye(h, dtype=w.dtype)
    return (wp[:, :, None, :] * eye[:, None, :, None]).reshape(h * LANES, h * LANES)


def _log_sigmoid(x):
    return jnp.minimum(x, 0.0) - jnp.log1p(jnp.exp(-jnp.abs(x)))


def _mlstm_kernel(mu_ref, mo_ref, gc_ref, gr_ref, cw_ref, cb_ref, wq_ref, wk_ref, wv_ref,
                  gbr_ref, gbc_ref, ng_ref, sk_ref, y_ref, uext_ref, c_ref, m_ref):
    L = MLSTM_CHUNK
    H = MLSTM_HEADS

    @pl.when(pl.program_id(1) == 0)
    def _():
        uext_ref[0:SUBLANES, :] = jnp.zeros((SUBLANES, MLSTM_PAD), f32)
        c_ref[...] = jnp.zeros_like(c_ref)
        m_ref[...] = jnp.zeros_like(m_ref)

    u = mu_ref[...]
    uext_ref[SUBLANES:SUBLANES + L, :] = u
    acc = jnp.broadcast_to(cb_ref[...], (L, MLSTM_PAD))
    for k in range(CONV_WIDTH):
        start = SUBLANES - (CONV_WIDTH - 1) + k
        acc = acc + cw_ref[k:k + 1, :] * uext_ref[start:start + L, :]
    c = acc * jax.nn.sigmoid(acc)
    uext_ref[0:SUBLANES, :] = u[L - SUBLANES:L, :]

    cb = c.astype(bf16)
    q = _dot(cb, wq_ref[...]).astype(bf16)
    kf = _dot(cb, wk_ref[...])
    vf = _dot(u.astype(bf16), wv_ref[...])

    gcol = gc_ref[...] + gbr_ref[...]
    grow = gr_ref[...] + gbc_ref[...]
    lf_col = _log_sigmoid(gcol)
    lf_row = _log_sigmoid(grow)
    ri = lax.broadcasted_iota(i32, (L, L), 0)
    ci = lax.broadcasted_iota(i32, (L, L), 1)
    causal = ci <= ri
    b_col = _dot_exact(causal.astype(f32), lf_col)
    b_row = _dot_exact(lf_row, (ri <= ci).astype(f32))
    lane = lax.broadcasted_iota(i32, (L, LANES), 1)
    o_gate = jax.nn.sigmoid(mo_ref[...])

    for h in range(H):
        sl = slice(h * LANES, (h + 1) * LANES)
        bc = b_col[:, H + h:H + h + 1]
        br = b_row[H + h:H + h + 1, :]
        lic = gcol[:, h:h + 1]
        lir = grow[h:h + 1, :]
        g = bc[L - 1:L, :]
        m_prev = m_ref[h][0:1, 0:1]
        c_prev = c_ref[h]

        d = jnp.where(causal, bc - br + lir, -jnp.inf)
        m_inter = bc + m_prev
        m_t = jnp.maximum(m_inter, jnp.max(d, axis=-1, keepdims=True))
        qh = q[:, sl]
        kh = kf[:, sl]
        v_aug = jnp.where(lane == HEAD_DIM, 1.0, vf[:, sl]).astype(bf16)
        w_intra = jnp.exp(d - m_t) * _dot_nt(qh, kh.astype(bf16))
        inter = jnp.exp(m_inter - m_t)
        nd = inter * _dot(qh, c_prev.astype(bf16)) + _dot(w_intra.astype(bf16), v_aug)
        den = nd[:, HEAD_DIM:HEAD_DIM + 1]
        hh = nd / jnp.maximum(jnp.abs(den), jnp.exp(-m_t))
        hh = jnp.where(lane < HEAD_DIM, hh, 0.0)
        hn = hh * lax.rsqrt(jnp.sum(hh * hh, axis=-1, keepdims=True) * (1.0 / HEAD_DIM) + RMS_EPS)
        y = o_gate[:, sl] * (hn * ng_ref[:, sl] + sk_ref[:, sl] * c[:, sl])
        y_ref[:, sl] = y.astype(y_ref.dtype)

        a = g - bc + lic
        m_new = jnp.maximum(g + m_prev, jnp.max(a, axis=0, keepdims=True))
        decay = jnp.exp(g + m_prev - m_new)
        kw = (kh * jnp.exp(a - m_new)).astype(bf16)
        c_ref[h] = decay * c_prev + _dot_tn(kw, v_aug)
        m_ref[h] = jnp.broadcast_to(m_new, (SUBLANES, LANES))


def _mlstm(mu, mo, gates, p, batch):
    t = mu.shape[0]
    s = t // batch
    L = MLSTM_CHUNK
    nc = s // L
    g_row = jnp.transpose(gates.reshape(batch, s, LANES)[:, :, :SUBLANES], (0, 2, 1))
    tok = lambda b, c: (b * nc + c, 0)
    w_spec = _full((MLSTM_PAD, MLSTM_PAD))
    v_spec = _full((1, MLSTM_PAD))
    return pl.pallas_call(
        _mlstm_kernel,
        grid=(batch, nc),
        in_specs=[pl.BlockSpec((L, MLSTM_PAD), tok), pl.BlockSpec((L, MLSTM_PAD), tok),
                  pl.BlockSpec((L, LANES), tok),
                  pl.BlockSpec((None, SUBLANES, L), lambda b, c: (b, 0, c)),
                  _full((CONV_WIDTH, MLSTM_PAD)), v_spec, w_spec, w_spec, w_spec,
                  _full((1, LANES)), _full((SUBLANES, 1)), v_spec, v_spec],
        out_specs=pl.BlockSpec((L, MLSTM_PAD), tok),
        out_shape=jax.ShapeDtypeStruct((t, MLSTM_PAD), bf16),
        scratch_shapes=[pltpu.VMEM((L + SUBLANES, MLSTM_PAD), f32),
                        pltpu.VMEM((MLSTM_HEADS, LANES, LANES), f32),
                        pltpu.VMEM((MLSTM_HEADS, SUBLANES, LANES), f32)],
        compiler_params=_cparams("arbitrary", "arbitrary"),
        name="mlstm",
    )(mu, mo, gates, g_row, p["conv_w"], p["conv_b"], p["wq"], p["wk"], p["wv"],
      p["gb_row"], p["gb_col"], p["norm"], p["skip"])


def _s5_param_kernel(are_ref, aim_ref, ldt_ref, bre_ref, bim_ref, pwr_ref, pwi_ref, bbr_ref, bbi_ref):
    a_re = are_ref[...]
    a_im = aim_ref[...]
    dt = jnp.exp(ldt_ref[...])
    mag = jnp.exp(a_re * dt)
    ab_re = mag * jnp.cos(a_im * dt)
    ab_im = mag * jnp.sin(a_im * dt)
    den = a_re * a_re + a_im * a_im
    xr, yi = ab_re - 1.0, ab_im
    coef_re = (xr * a_re + yi * a_im) / den
    coef_im = (yi * a_re - xr * a_im) / den
    b_re = bre_ref[...]
    b_im = bim_ref[...]
    bbr_ref[...] = coef_re[:, None, :] * b_re - coef_im[:, None, :] * b_im
    bbi_ref[...] = coef_re[:, None, :] * b_im + coef_im[:, None, :] * b_re
    pr, pi = ab_re, ab_im
    pwr_ref[0] = pr
    pwi_ref[0] = pi
    for k in range(1, SUBLANES):
        pr, pi = pr * ab_re - pi * ab_im, pr * ab_im + pi * ab_re
        pwr_ref[k] = pr
        pwi_ref[k] = pi


def _s5_params(a_re, a_im, log_dt, b_re, b_im):
    G, P, Hc = S5_GROUPS, S5_STATE, S5_GROUP_CH
    bt_re = jnp.transpose(b_re, (0, 2, 1))
    bt_im = jnp.transpose(b_im, (0, 2, 1))
    return pl.pallas_call(
        _s5_param_kernel,
        out_shape=[jax.ShapeDtypeStruct((SUBLANES, G, P), f32), jax.ShapeDtypeStruct((SUBLANES, G, P), f32),
                   jax.ShapeDtypeStruct((G, Hc, P), f32), jax.ShapeDtypeStruct((G, Hc, P), f32)],
        name="s5_params",
    )(a_re, a_im, log_dt.reshape(G, 1), bt_re, bt_im)


def _gelu_tanh(x):
    return 0.5 * x * (1.0 + jnp.tanh(math.sqrt(2.0 / math.pi) * (x + 0.044715 * (x * x * x))))


def _s5_kernel(u_ref, wb_ref, pwr_ref, pwi_ref, wc_ref, dsk_ref, wg_ref, bg_ref, ng_ref, y_ref,
               sr_ref, si_ref, cr_ref, ci_ref):
    TT = u_ref.shape[0]
    NL = S5_LANES
    ng8 = TT // SUBLANES

    @pl.when(pl.program_id(1) == 0)
    def _():
        cr_ref[...] = jnp.zeros_like(cr_ref)
        ci_ref[...] = jnp.zeros_like(ci_ref)

    u = u_ref[...]
    bu = _dot(u.astype(bf16), wb_ref[...])
    xr = bu[:, :NL].reshape(ng8, SUBLANES, NL)
    xi = bu[:, NL:].reshape(ng8, SUBLANES, NL)
    row = lax.broadcasted_iota(i32, (1, SUBLANES, NL), 1)
    for sh in (1, 2, 4):
        ar = pwr_ref[sh - 1:sh, :][None]
        ai = pwi_ref[sh - 1:sh, :][None]
        tr = pltpu.roll(xr, sh, 1)
        ti = pltpu.roll(xi, sh, 1)
        keep = row >= sh
        xr, xi = (xr + jnp.where(keep, ar * tr - ai * ti, 0.0),
                  xi + jnp.where(keep, ar * ti + ai * tr, 0.0))
    sr_ref[...] = xr.reshape(TT, NL)
    si_ref[...] = xi.reshape(TT, NL)

    pr = pwr_ref[...]
    pi = pwi_ref[...]

    def group(gidx, carry):
        c_re, c_im = carry
        r0 = pl.multiple_of(gidx * SUBLANES, SUBLANES)
        t_re = sr_ref[pl.ds(r0, SUBLANES), :] + pr * c_re - pi * c_im
        t_im = si_ref[pl.ds(r0, SUBLANES), :] + pr * c_im + pi * c_re
        sr_ref[pl.ds(r0, SUBLANES), :] = t_re
        si_ref[pl.ds(r0, SUBLANES), :] = t_im
        return t_re[SUBLANES - 1:SUBLANES, :], t_im[SUBLANES - 1:SUBLANES, :]

    c_re, c_im = lax.fori_loop(0, ng8, group, (cr_ref[...], ci_ref[...]))
    cr_ref[...] = c_re
    ci_ref[...] = c_im

    y = (_dot(sr_ref[...].astype(bf16), wc_ref[0:NL, :]) + _dot(si_ref[...].astype(bf16), wc_ref[NL:2 * NL, :])
         + dsk_ref[...] * u)
    z = _gelu_tanh(y)
    out = z * jax.nn.sigmoid(_dot(z.astype(bf16), wg_ref[...]) + bg_ref[...])
    y_ref[...] = _rms(out, ng_ref[...]).astype(y_ref.dtype)


def _s5(su, p, batch):
    t = su.shape[0]
    s = t // batch
    tt = min(S5_BLOCK, s)
    nb = s // tt
    W, NL = S5_WIDTH, S5_LANES
    tok = lambda b, c: (b * nb + c, 0)
    return pl.pallas_call(
        _s5_kernel,
        grid=(batch, nb),
        in_specs=[pl.BlockSpec((tt, W), tok), _full((W, 2 * NL)), _full((SUBLANES, NL)), _full((SUBLANES, NL)),
                  _full((2 * NL, W)), _full((1, W)), _full((W, W)), _full((1, W)), _full((1, W))],
        out_specs=pl.BlockSpec((tt, W), tok),
        out_shape=jax.ShapeDtypeStruct((t, W), bf16),
        scratch_shapes=[pltpu.VMEM((tt, NL), f32), pltpu.VMEM((tt, NL), f32),
                        pltpu.VMEM((1, NL), f32), pltpu.VMEM((1, NL), f32)],
        compiler_params=_cparams("arbitrary", "arbitrary"),
        name="s5",
    )(su, p["wb"], p["pw_re"], p["pw_im"], p["wc"], p["d"], p["w_glu"], p["b_glu"], p["norm"])


def _t5_bucket(dist):
    exact = dist < REL_MAX_EXACT
    large = REL_MAX_EXACT + (np.log(np.maximum(dist, 1) / REL_MAX_EXACT)
                             / np.log(REL_MAX_DIST / REL_MAX_EXACT)
                             * (REL_BUCKETS - REL_MAX_EXACT)).astype(np.int32)
    large = np.minimum(large, REL_BUCKETS - 1)
    return np.where(exact, dist, large).astype(np.int32)


def _band_offsets():
    i = np.arange(ATTN_BLOCK)[:, None]
    j = np.arange(2 * ATTN_BLOCK)[None, :]
    return ATTN_BLOCK + i - j


def _attn_bias(rel_table, dil):
    bucket = _t5_bucket(np.clip(_band_offsets(), 0, None) * dil)
    return jnp.transpose(rel_table.astype(f32)[bucket], (2, 0, 1))


def _attn_kernel(q_ref, kc_ref, kp_ref, vc_ref, vp_ref, bias_ref, o_ref, lse_ref, kcat_ref, vcat_ref, *,
                 n_back):
    RB = q_ref.shape[0]
    BLK = ATTN_BLOCK
    n = pl.program_id(2)
    kcat_ref[0:BLK, :] = kp_ref[...]
    kcat_ref[BLK:BLK + RB, :] = kc_ref[...]
    vcat_ref[0:BLK, :] = vp_ref[...]
    vcat_ref[BLK:BLK + RB, :] = vc_ref[...]

    ri = lax.broadcasted_iota(i32, (BLK, 2 * BLK), 0)
    ci = lax.broadcasted_iota(i32, (BLK, 2 * BLK), 1)
    off = BLK + ri - ci
    band = (off >= 0) & (off <= n_back)
    lane_q = lax.broadcasted_iota(i32, (BLK, LANES), 1)
    lane_kv = lax.broadcasted_iota(i32, (2 * BLK, LANES), 1)

    def sub_block(j, _):
        r0 = pl.multiple_of(j * BLK, BLK)
        has_prev = (n > 0) | (j > 0)
        valid = band & (has_prev | (ci >= BLK))
        for hp in range(ATTN_HEADS // 2):
            cs = slice(hp * LANES, (hp + 1) * LANES)
            qp = q_ref[pl.ds(r0, BLK), cs]
            kp = kcat_ref[pl.ds(r0, 2 * BLK), cs]
            vp = vcat_ref[pl.ds(r0, 2 * BLK), cs]
            o_pair = None
            lse_pair = None
            for e in range(2):
                in_head_q = (lane_q >= HEAD_DIM) == (e == 1)
                in_head_kv = (lane_kv >= HEAD_DIM) == (e == 1)
                qm = jnp.where(in_head_q, qp, jnp.zeros_like(qp))
                s = _dot_nt(qm, kp) + bias_ref[2 * hp + e]
                s = jnp.where(valid, s, NEG_INF)
                m = jnp.max(s, axis=-1, keepdims=True)
                pexp = jnp.exp(s - m)
                l = jnp.sum(pexp, axis=-1, keepdims=True)
                vm = jnp.where(in_head_kv, vp, jnp.zeros_like(vp))
                o_e = _dot((pexp / l).astype(bf16), vm)
                lse_e = jnp.broadcast_to(m + jnp.log(l), (BLK, LANES))
                o_pair = o_e if o_pair is None else o_pair + o_e
                lse_pair = lse_e if lse_pair is None else jnp.where(in_head_q, lse_e, lse_pair)
            o_ref[pl.ds(r0, BLK), cs] = o_pair
            lse_ref[pl.ds(r0, BLK), cs] = lse_pair
        return 0

    lax.fori_loop(0, RB // BLK, sub_block, 0)


def _dilated_attn(q, k, v, bias, window, dil, batch):
    t, w = q.shape
    s = t // batch
    n = s // dil
    rb = min(TOKEN_BLOCK, n)
    nblk = n // rb
    per = rb // ATTN_BLOCK
    view = lambda a: a.reshape(batch, n, dil * w)
    cur = pl.BlockSpec((None, rb, w), lambda b, r, i: (b, i, r))
    prev = pl.BlockSpec((None, ATTN_BLOCK, w), lambda b, r, i: (b, jnp.maximum(i * per - 1, 0), r))
    out = pl.BlockSpec((None, rb, w), lambda b, r, i: (b, i, r))
    o, lse = pl.pallas_call(
        functools.partial(_attn_kernel, n_back=window // dil),
        grid=(batch, dil, nblk),
        in_specs=[cur, cur, prev, cur, prev, _full(bias.shape)],
        out_specs=[out, out],
        out_shape=[jax.ShapeDtypeStruct((batch, n, dil * w), f32)] * 2,
        scratch_shapes=[pltpu.VMEM((rb + ATTN_BLOCK, w), bf16), pltpu.VMEM((rb + ATTN_BLOCK, w), bf16)],
        compiler_params=_cparams("parallel", "parallel", "arbitrary"),
        name=f"dilated_attn_{dil}",
    )(view(q), view(k), view(k), view(v), view(v), bias)
    return o.reshape(t, w), lse.reshape(t, w)


def _out_proj_kernel(x_ref, ym_ref, ys_ref, o1_ref, o2_ref, o3_ref, l1_ref, l2_ref, l3_ref, ag_ref,
                     wm_ref, ws_ref, wa_ref, out_ref):
    l1, l2, l3 = l1_ref[...], l2_ref[...], l3_ref[...]
    mx = jnp.maximum(jnp.maximum(l1, l2), l3)
    e1, e2, e3 = jnp.exp(l1 - mx), jnp.exp(l2 - mx), jnp.exp(l3 - mx)
    tot = e1 + e2 + e3
    ya = (e1 / tot) * o1_ref[...] + (e2 / tot) * o2_ref[...] + (e3 / tot) * o3_ref[...]
    ya = _rms(ya, ag_ref[...]).astype(bf16)
    out_ref[...] = (x_ref[...] + _dot(ym_ref[...], wm_ref[...]) + _dot(ys_ref[...], ws_ref[...])
                    + _dot(ya, wa_ref[...]))


def _out_proj(x2, ym, ys, attn, a_gain, wm, ws, wa):
    t, d = x2.shape
    tm = min(TOKEN_BLOCK, t)
    row = lambda w: pl.BlockSpec((tm, w), lambda i: (i, 0))
    (o1, l1), (o2, l2), (o3, l3) = attn
    return pl.pallas_call(
        _out_proj_kernel,
        grid=(t // tm,),
        in_specs=[row(d), row(MLSTM_PAD), row(S5_WIDTH)] + [row(ATTN_WIDTH)] * 6
                 + [_full((1, ATTN_WIDTH)), _full(wm.shape), _full(ws.shape), _full(wa.shape)],
        out_specs=row(d),
        out_shape=jax.ShapeDtypeStruct((t, d), f32),
        compiler_params=_cparams("parallel"),
        name="out_proj",
    )(x2, ym, ys, o1, o2, o3, l1, l2, l3, a_gain, wm, ws, wa)


def _mem_kv_kernel(mem_ref, g_ref, w_ref, kv_ref):
    kv_ref[...] = _dot(_rms(mem_ref[...], g_ref[...]).astype(bf16), w_ref[...]).astype(kv_ref.dtype)


def _mem_kv(mem2, gain, w_kv):
    r, d = mem2.shape
    return pl.pallas_call(
        _mem_kv_kernel,
        out_shape=jax.ShapeDtypeStruct((r, 2 * d), bf16),
        compiler_params=pltpu.CompilerParams(vmem_limit_bytes=VMEM_LIMIT),
        name="mem_kv",
    )(mem2, gain, w_kv)


def _xattn_kernel(x_ref, g_ref, wq_ref, kv_ref, wo_ref, out_ref):
    x = x_ref[...]
    d = x.shape[1]
    hd = d // MEM_HEADS
    q = _dot(_rms(x, g_ref[...]).astype(bf16), wq_ref[...]).astype(bf16)
    heads = []
    for h in range(MEM_HEADS):
        kh = kv_ref[:, h * hd:(h + 1) * hd]
        vh = kv_ref[:, d + h * hd:d + (h + 1) * hd]
        s = _dot_nt(q[:, h * hd:(h + 1) * hd], kh) * (hd ** -0.5)
        m = jnp.max(s, axis=-1, keepdims=True)
        p = jnp.exp(s - m)
        p = p / jnp.sum(p, axis=-1, keepdims=True)
        heads.append(_dot(p.astype(bf16), vh).astype(bf16))
    o = jnp.concatenate(heads, axis=-1)
    out_ref[...] = x + _dot(o, wo_ref[...])


def _xattn(x2, gain, w_q, kv, w_o, batch):
    t, d = x2.shape
    s = t // batch
    m = kv.shape[0] // batch
    tm = min(TOKEN_BLOCK, s)
    nb = s // tm
    tok = lambda b, i: (b * nb + i, 0)
    return pl.pallas_call(
        _xattn_kernel,
        grid=(batch, nb),
        in_specs=[pl.BlockSpec((tm, d), tok), _full((1, d)), _full((d, d)),
                  pl.BlockSpec((m, 2 * d), lambda b, i: (b, 0)), _full((d, d))],
        out_specs=pl.BlockSpec((tm, d), tok),
        out_shape=jax.ShapeDtypeStruct((t, d), f32),
        compiler_params=_cparams("parallel", "parallel"),
        name="xattn",
    )(x2, gain, w_q, kv, w_o)


def _first_argmax(x, lane):
    m = jnp.max(x, axis=-1, keepdims=True)
    idx = jnp.min(jnp.where(x == m, lane, LANES), axis=-1, keepdims=True)
    return m, idx


def _router_kernel(x_ref, g_ref, wr_ref, br_ref, xn_ref, sel_ref, gate_ref, cnt_ref, run_ref):
    TM = x_ref.shape[0]
    G, EPG, E = MOE_GROUPS, EXPERTS_PER_GROUP, N_EXPERTS

    @pl.when(pl.program_id(0) == 0)
    def _():
        run_ref[...] = jnp.zeros_like(run_ref)

    xn = _rms(x_ref[...], g_ref[...])
    xn_ref[...] = xn
    logits = _dot_exact(xn, wr_ref[...]) + br_ref[...]
    lane = lax.broadcasted_iota(i32, (TM, LANES), 1)
    is_group = (lane >= E) & (lane < E + G)
    gl = jnp.where(is_group, logits, -jnp.inf)
    gmax, gidx = _first_argmax(gl, lane)
    g_p = 1.0 / jnp.sum(jnp.exp(gl - gmax), axis=-1, keepdims=True)
    g_i = gidx - E
    in_group = (lane < E) & ((lane >> int(math.log2(EPG))) == g_i)
    el = jnp.where(in_group, logits, -jnp.inf)
    m1, i1 = _first_argmax(el, lane)
    m2, i2 = _first_argmax(jnp.where(lane == i1, -jnp.inf, el), lane)
    p2 = jnp.exp(m2 - m1)
    gate1 = g_p / (1.0 + p2)
    gate2 = g_p * p2 / (1.0 + p2)

    onehot = ((lane == i1) | (lane == i2)).astype(bf16)
    r = lax.broadcasted_iota(i32, (TM, TM), 0)
    c = lax.broadcasted_iota(i32, (TM, TM), 1)
    before = _dot((c < r).astype(bf16), onehot) + run_ref[0:1, :]
    rank1 = jnp.sum(jnp.where(lane == i1, before, 0.0), axis=-1, keepdims=True)
    rank2 = jnp.sum(jnp.where(lane == i2, before, 0.0), axis=-1, keepdims=True)
    run_ref[...] = run_ref[...] + jnp.sum(onehot.astype(f32), axis=0, keepdims=True)
    cnt_ref[...] = run_ref[...]

    sel = jnp.where(lane == 0, i1, jnp.where(lane == 1, i2, 0))
    sel = jnp.where(lane == 2, rank1.astype(i32), jnp.where(lane == 3, rank2.astype(i32), sel))
    sel_ref[...] = sel
    gate_ref[...] = jnp.where(lane == 0, gate1, jnp.where(lane == 1, gate2, 0.0))


def _router(x2, gain, w_r, b_r):
    t, d = x2.shape
    tm = min(TOKEN_BLOCK, t)
    row = lambda w: pl.BlockSpec((tm, w), lambda i: (i, 0))
    return pl.pallas_call(
        _router_kernel,
        grid=(t // tm,),
        in_specs=[row(d), _full((1, d)), _full((d, LANES)), _full((1, LANES))],
        out_specs=[row(d), row(LANES), row(LANES), _full((SUBLANES, LANES))],
        out_shape=[jax.ShapeDtypeStruct((t, d), f32), jax.ShapeDtypeStruct((t, LANES), i32),
                   jax.ShapeDtypeStruct((t, LANES), f32), jax.ShapeDtypeStruct((SUBLANES, LANES), f32)],
        scratch_shapes=[pltpu.VMEM((SUBLANES, LANES), f32)],
        compiler_params=_cparams("arbitrary"),
        name="moe_router",
    )(x2, gain, w_r, b_r)


def _row_copy(src_ref, src_row, dst_ref, dst_row, sem):
    return pltpu.make_async_copy(src_ref.at[pl.ds(src_row, 1), :], dst_ref.at[pl.ds(dst_row, 1), :], sem)


def _dispatch_kernel(dest_ref, xn_ref, rows_in_ref, rows_ref, sem):
    del rows_in_ref
    TM = xn_ref.shape[0]
    base = pl.program_id(0) * (2 * TM)

    def issue(a, _):
        _row_copy(xn_ref, a // 2, rows_ref, dest_ref[base + a], sem).start()
        return 0

    lax.fori_loop(0, 2 * TM, issue, 0)

    def drain(a, _):
        _row_copy(xn_ref, a // 2, rows_ref, dest_ref[base + a], sem).wait()
        return 0

    lax.fori_loop(0, 2 * TM, drain, 0)


def _dispatch(dest, xn, n_rows):
    t, d = xn.shape
    tm = min(TOKEN_BLOCK, t)
    rows0 = jnp.zeros((n_rows, d), f32)
    return pl.pallas_call(
        _dispatch_kernel,
        grid_spec=pltpu.PrefetchScalarGridSpec(
            num_scalar_prefetch=1,
            grid=(t // tm,),
            in_specs=[pl.BlockSpec((tm, d), lambda i, dest: (i, 0)), pl.BlockSpec(memory_space=pl.ANY)],
            out_specs=pl.BlockSpec(memory_space=pl.ANY),
            scratch_shapes=[pltpu.SemaphoreType.DMA(())],
        ),
        out_shape=jax.ShapeDtypeStruct((n_rows, d), f32),
        input_output_aliases={2: 0},
        compiler_params=_cparams("arbitrary"),
        name="moe_dispatch",
    )(dest, xn, rows0)


def _expert_kernel(be_ref, nu_ref, x_ref, wgu_ref, wd_ref, y_ref):
    i = pl.program_id(0)
    de = wd_ref.shape[0]

    @pl.when(i < nu_ref[0])
    def _():
        ab = _dot(x_ref[...].astype(bf16), wgu_ref[...])
        a, b = ab[:, :de], ab[:, de:]
        hid = (a * jax.nn.sigmoid(a) * b).astype(bf16)
        y_ref[...] = _dot(hid, wd_ref[...])

    @pl.when(i >= nu_ref[0])
    def _():
        y_ref[...] = jnp.zeros_like(y_ref)


def _experts(block_e, n_used, rows, w_gu, w_down):
    n_rows, d = rows.shape
    nblk = n_rows // MOE_ROWS
    de = w_down.shape[1]
    return pl.pallas_call(
        _expert_kernel,
        grid_spec=pltpu.PrefetchScalarGridSpec(
            num_scalar_prefetch=2,
            grid=(nblk,),
            in_specs=[pl.BlockSpec((MOE_ROWS, d), lambda i, be, nu: (i, 0)),
                      pl.BlockSpec((None, d, 2 * de), lambda i, be, nu: (be[i], 0, 0)),
                      pl.BlockSpec((None, de, d), lambda i, be, nu: (be[i], 0, 0))],
            out_specs=pl.BlockSpec((MOE_ROWS, d), lambda i, be, nu: (i, 0)),
        ),
        out_shape=jax.ShapeDtypeStruct((n_rows, d), f32),
        compiler_params=_cparams("arbitrary"),
        name="moe_experts",
    )(block_e, n_used, rows, w_gu, w_down)


def _combine_kernel(dest_ref, x_ref, gate_ref, fg_ref, y_hbm, out_ref, buf0, buf1, sem, *, final_norm):
    TM = x_ref.shape[0]
    base = pl.program_id(0) * (2 * TM)

    def issue(tk, _):
        _row_copy(y_hbm, dest_ref[base + 2 * tk], buf0, tk, sem).start()
        _row_copy(y_hbm, dest_ref[base + 2 * tk + 1], buf1, tk, sem).start()
        return 0

    lax.fori_loop(0, TM, issue, 0)

    def drain(tk, _):
        _row_copy(y_hbm, dest_ref[base + 2 * tk], buf0, tk, sem).wait()
        _row_copy(y_hbm, dest_ref[base + 2 * tk + 1], buf1, tk, sem).wait()
        return 0

    lax.fori_loop(0, TM, drain, 0)
    gates = gate_ref[...]
    out = x_ref[...] + gates[:, 0:1] * buf0[...] + gates[:, 1:2] * buf1[...]
    if final_norm:
        out = _rms(out, fg_ref[...])
    out_ref[...] = out


def _combine(dest, x2, gates, y_rows, final_gain, final_norm):
    t, d = x2.shape
    tm = min(TOKEN_BLOCK, t)
    return pl.pallas_call(
        functools.partial(_combine_kernel, final_norm=final_norm),
        grid_spec=pltpu.PrefetchScalarGridSpec(
            num_scalar_prefetch=1,
            grid=(t // tm,),
            in_specs=[pl.BlockSpec((tm, d), lambda i, dest: (i, 0)),
                      pl.BlockSpec((tm, LANES), lambda i, dest: (i, 0)),
                      pl.BlockSpec((1, d), lambda i, dest: (0, 0)),
                      pl.BlockSpec(memory_space=pl.ANY)],
            out_specs=pl.BlockSpec((tm, d), lambda i, dest: (i, 0)),
            scratch_shapes=[pltpu.VMEM((tm, d), f32), pltpu.VMEM((tm, d), f32), pltpu.SemaphoreType.DMA(())],
        ),
        out_shape=jax.ShapeDtypeStruct((t, d), f32),
        compiler_params=_cparams("arbitrary"),
        name="moe_combine",
    )(dest, x2, gates, final_gain, y_rows)


def _moe(x2, gain, w_r, b_r, w_gu, w_down, final_gain, final_norm):
    t, d = x2.shape
    xn, sel, gates, counts = _router(x2, gain, w_r, b_r)
    counts = counts[0, :N_EXPERTS].astype(i32)
    padded = (counts + MOE_ROWS - 1) // MOE_ROWS * MOE_ROWS
    p_ends = jnp.cumsum(padded)
    p_starts = p_ends - padded
    n_blocks = (2 * t) // MOE_ROWS + N_EXPERTS
    dest = (p_starts[sel[:, 0:2]] + sel[:, 2:4]).reshape(-1).astype(i32)
    block_e = jnp.minimum(jnp.searchsorted(p_ends, jnp.arange(n_blocks, dtype=i32) * MOE_ROWS, side='right'),
                          N_EXPERTS - 1).astype(i32)
    n_used = (p_ends[-1:] // MOE_ROWS).astype(i32)
    rows = _dispatch(dest, xn, n_blocks * MOE_ROWS)
    y_rows = _experts(block_e, n_used, rows, w_gu, w_down)
    return _combine(dest, x2, gates, y_rows, final_gain, final_norm)


def _layer_params(l, a):
    H, W = MLSTM_HEADS, MLSTM_WIDTH
    w_in = a["w_in"][l]
    sizes = [W, W, H, H, S5_WIDTH, ATTN_WIDTH, ATTN_WIDTH, ATTN_WIDTH]
    offs = np.concatenate([[0], np.cumsum(sizes)])
    seg = lambda i: w_in[:, offs[i]:offs[i + 1]]
    gates_w = jnp.pad(jnp.concatenate([seg(2), seg(3)], axis=1), ((0, 0), (0, LANES - 2 * H)))
    w_cat = jnp.concatenate([_pad_heads(seg(0), 1), _pad_heads(seg(1), 1), gates_w, seg(4),
                             seg(5) * (HEAD_DIM ** -0.5), seg(6), seg(7)], axis=1).astype(bf16)
    row = lambda v: v.reshape(1, -1)
    gb = a["mlstm_gate_bias"][l]
    mlstm = dict(
        conv_w=_pad_heads(a["mlstm_conv_w"][l], 1), conv_b=row(_pad_heads(a["mlstm_conv_b"][l], 0)),
        wq=_block_diag_heads(a["mlstm_wq"][l] * (HEAD_DIM ** -0.5)).astype(bf16),
        wk=_block_diag_heads(a["mlstm_wk"][l]).astype(bf16),
        wv=_block_diag_heads(a["mlstm_wv"][l]).astype(bf16),
        gb_row=jnp.pad(gb, (0, LANES - 2 * H)).reshape(1, LANES), gb_col=gb.reshape(2 * H, 1),
        norm=row(_pad_heads(a["mlstm_norm"][l], 0)), skip=row(_pad_heads(a["mlstm_skip"][l], 0)))

    G, P, Hc = S5_GROUPS, S5_STATE, S5_GROUP_CH
    pw_re, pw_im, bb_re, bb_im = _s5_params(a["s5_a_re"][l], a["s5_a_im"][l], a["s5_log_dt"][l],
                                            a["s5_b_re"][l], a["s5_b_im"][l])
    eye = jnp.eye(G, dtype=f32)
    bd_in = lambda m: (m[:, :, None, :] * eye[:, None, :, None]).reshape(G * Hc, G * P)
    bd_out = lambda m: (jnp.transpose(m, (0, 2, 1))[:, :, None, :] * eye[:, None, :, None]).reshape(G * P, G * Hc)
    s5 = dict(
        wb=jnp.concatenate([bd_in(bb_re), bd_in(bb_im)], axis=1).astype(bf16),
        wc=jnp.concatenate([bd_out(a["s5_c_re"][l]), -bd_out(a["s5_c_im"][l])], axis=0).astype(bf16),
        pw_re=pw_re.reshape(SUBLANES, G * P), pw_im=pw_im.reshape(SUBLANES, G * P),
        d=a["s5_d"][l].reshape(1, S5_WIDTH), w_glu=a["s5_w_glu"][l].astype(bf16),
        b_glu=row(a["s5_b_glu"][l]), norm=row(a["s5_out_norm"][l]))

    w_out = a["w_out"][l]
    w_r = jnp.pad(jnp.concatenate([a["router_w_expert"][l], a["router_w_group"][l]], axis=1),
                  ((0, 0), (0, LANES - N_EXPERTS - MOE_GROUPS)))
    b_r = jnp.pad(jnp.concatenate([a["router_b_expert"][l], a["router_b_group"][l]]),
                  (0, LANES - N_EXPERTS - MOE_GROUPS)).reshape(1, LANES)
    return dict(
        norm_mix=row(a["norm_mix"][l]), w_cat=w_cat, mlstm=mlstm, s5=s5,
        attn_norm=row(a["attn_out_norm"][l]),
        wo_m=_pad_heads(w_out[:W], 0).astype(bf16), wo_s=w_out[W:W + S5_WIDTH].astype(bf16),
        wo_a=w_out[W + S5_WIDTH:].astype(bf16),
        norm_xattn=row(a["norm_xattn"][l]), norm_mem=row(a["norm_mem"][l]),
        xq=a["xattn_w_q"][l].astype(bf16), xkv=a["xattn_w_kv"][l].astype(bf16), xo=a["xattn_w_o"][l].astype(bf16),
        norm_ffn=row(a["norm_ffn"][l]), w_r=w_r, b_r=b_r,
        w_gu=a["expert_w_gu"][l].astype(bf16), w_down=a["expert_w_down"][l].astype(bf16))


def _forward(a):
    x = a["x"]
    batch, seq, d = x.shape
    depth = a["w_in"].shape[0]
    x2 = x.reshape(batch * seq, d)
    mem2 = a["mem"].reshape(-1, d)
    biases = [_attn_bias(a["rel_bias"], dil) for _, dil in DILATED_PATTERNS]
    final_gain = a["final_norm"].reshape(1, d)
    for l in range(depth):
        p = _layer_params(l, a)
        mu, mo, gates, su, q, k, v = _in_proj(x2, p["norm_mix"], p["w_cat"])
        ym = _mlstm(mu, mo, gates, p["mlstm"], batch)
        ys = _s5(su, p["s5"], batch)
        attn = [_dilated_attn(q, k, v, bias, window, dil, batch)
                for bias, (window, dil) in zip(biases, DILATED_PATTERNS)]
        x2 = _out_proj(x2, ym, ys, attn, p["attn_norm"], p["wo_m"], p["wo_s"], p["wo_a"])
        kv = _mem_kv(mem2, p["norm_mem"], p["xkv"])
        x2 = _xattn(x2, p["norm_xattn"], p["xq"], kv, p["xo"], batch)
        x2 = _moe(x2, p["norm_ffn"], p["w_r"], p["b_r"], p["w_gu"], p["w_down"], final_gain,
                  final_norm=(l == depth - 1))
    return x2.reshape(batch, seq, d)


def kernel(x, mem, rel_bias, norm_mix, w_in, mlstm_conv_w, mlstm_conv_b, mlstm_wq, mlstm_wk, mlstm_wv,
           mlstm_gate_bias, mlstm_norm, mlstm_skip, s5_a_re, s5_a_im, s5_log_dt, s5_b_re, s5_b_im, s5_c_re,
           s5_c_im, s5_d, s5_w_glu, s5_b_glu, s5_out_norm, attn_out_norm, w_out, norm_xattn, norm_mem,
           xattn_w_q, xattn_w_kv, xattn_w_o, norm_ffn, router_w_group, router_b_group, router_w_expert,
           router_b_expert, expert_w_gu, expert_w_down, final_norm):
    return _forward(dict(locals()))
```

```python
import functools
import math

import numpy as np
import jax
import jax.numpy as jnp
from jax import lax
from jax.experimental import pallas as pl
from jax.experimental.pallas import tpu as pltpu

f32 = jnp.float32
bf16 = jnp.bfloat16
i32 = jnp.int32

HEAD_DIM = 64
MLSTM_HEADS = 4
MLSTM_WIDTH = MLSTM_HEADS * HEAD_DIM
CONV_WIDTH = 4
S5_GROUPS = 16
S5_GROUP_CH = 16
S5_STATE = 64
S5_WIDTH = S5_GROUPS * S5_GROUP_CH
S5_LANES = S5_GROUPS * S5_STATE
ATTN_HEADS = 8
ATTN_WIDTH = ATTN_HEADS * HEAD_DIM
ATTN_BLOCK = 128
DILATED_PATTERNS = ((128, 1), (512, 4), (2048, 16))
REL_BUCKETS = 32
REL_MAX_EXACT = 16
REL_MAX_DIST = 2048
MEM_HEADS = 4
MOE_GROUPS = 4
EXPERTS_PER_GROUP = 8
N_EXPERTS = MOE_GROUPS * EXPERTS_PER_GROUP
RMS_EPS = 1e-6
NEG_INF = -1e30

LANES = 128
SUBLANES = 8
VMEM_LIMIT = 48 * 1024 * 1024

MLSTM_PAD = MLSTM_HEADS * LANES
MLSTM_CHUNK = 128
S5_BLOCK = 256
MOE_ROWS = 256
TOKEN_BLOCK = 512


def _cparams(*sem):
    return pltpu.CompilerParams(dimension_semantics=sem, vmem_limit_bytes=VMEM_LIMIT)


def _rms(x, gain):
    return x * lax.rsqrt(jnp.mean(x * x, axis=-1, keepdims=True) + RMS_EPS) * gain


def _dot(a, b):
    return jnp.dot(a, b, preferred_element_type=f32)


def _dot_nt(a, b):
    return lax.dot_general(a, b, (((1,), (1,)), ((), ())), preferred_element_type=f32)


def _dot_tn(a, b):
    return lax.dot_general(a, b, (((0,), (0,)), ((), ())), preferred_element_type=f32)


def _dot_exact(a, b):
    return jnp.dot(a, b, preferred_element_type=f32, precision=lax.Precision.HIGHEST)


def _full(shape):
    n = len(shape)
    return pl.BlockSpec(shape, lambda *_: (0,) * n)


IN_SEGS = (MLSTM_PAD, MLSTM_PAD, LANES, S5_WIDTH, ATTN_WIDTH, ATTN_WIDTH, ATTN_WIDTH)


DILATIONS = tuple(dil for _, dil in DILATED_PATTERNS)
ATTN_TILES = ATTN_WIDTH // LANES


def _in_proj_kernel(x_ref, g_ref, w_ref, mu_ref, mo_ref, gt_ref, su_ref, *rest):
    qkv_refs, scr_ref = rest[:-1], rest[-1]
    tm = x_ref.shape[0]
    xn = _rms(x_ref[...], g_ref[...]).astype(bf16)
    off = 0
    for o_ref, width in zip((mu_ref, mo_ref, gt_ref, su_ref), IN_SEGS[:4]):
        o_ref[...] = _dot(xn, w_ref[:, off:off + width]).astype(o_ref.dtype)
        off += width
    nd = len(DILATIONS)
    for a in range(3):
        val = _dot(xn, w_ref[:, off:off + ATTN_WIDTH])
        off += ATTN_WIDTH
        for c in range(ATTN_TILES):
            scr_ref[a, c] = val[:, c * LANES:(c + 1) * LANES]
        for o_ref, dil in zip(qkv_refs[a * nd:(a + 1) * nd], DILATIONS):
            if dil == 1:
                o_ref[...] = val.astype(o_ref.dtype)
                continue
            for r in range(dil):
                for c in range(ATTN_TILES):
                    col = r * ATTN_WIDTH + c * LANES
                    o_ref[:, col:col + LANES] = scr_ref[a, c, pl.ds(r, tm // dil, stride=dil), :].astype(o_ref.dtype)


def _in_proj(x2, gain, w_cat):
    t, d = x2.shape
    tm = min(TOKEN_BLOCK, t)
    widths = IN_SEGS[:4]
    shapes = [(t, w, tm, f32) for w in widths]
    shapes += [(t // dil, dil * ATTN_WIDTH, tm // dil, bf16) for _ in range(3) for dil in DILATIONS]
    return pl.pallas_call(
        _in_proj_kernel,
        grid=(t // tm,),
        in_specs=[pl.BlockSpec((tm, d), lambda i: (i, 0)), _full((1, d)), _full(w_cat.shape)],
        out_specs=[pl.BlockSpec((rows, w), lambda i: (i, 0)) for _, w, rows, _ in shapes],
        out_shape=[jax.ShapeDtypeStruct((n, w), dt) for n, w, _, dt in shapes],
        scratch_shapes=[pltpu.VMEM((3, ATTN_TILES, tm, LANES), f32)],
        compiler_params=_cparams("parallel"),
        name="in_proj",
    )(x2, gain, w_cat)


def _pad_heads(w, axis):
    shape = w.shape
    h = shape[axis] // HEAD_DIM
    w = w.reshape(shape[:axis] + (h, HEAD_DIM) + shape[axis + 1:])
    pad = [(0, 0)] * w.ndim
    pad[axis + 1] = (0, LANES - HEAD_DIM)
    w = jnp.pad(w, pad)
    return w.reshape(shape[:axis] + (h * LANES,) + shape[axis + 1:])


def _block_diag_heads(w):
    h = w.shape[0]
    wp = jnp.pad(w, ((0, 0), (0, LANES - HEAD_DIM), (0, LANES - HEAD_DIM)))
    eye = jnp.eye(h, dtype=w.dtype)
    return (wp[:, :, None, :] * eye[:, None, :, None]).reshape(h * LANES, h * LANES)


def _log_sigmoid(x):
    return jnp.minimum(x, 0.0) - jnp.log1p(jnp.exp(-jnp.abs(x)))


def _mlstm_kernel(mu_ref, mo_ref, gc_ref, gr_ref, cw_ref, cb_ref, wq_ref, wk_ref, wv_ref,
                  gbr_ref, gbc_ref, ng_ref, sk_ref, y_ref, uext_ref, c_ref, m_ref):
    L = MLSTM_CHUNK
    H = MLSTM_HEADS

    @pl.when(pl.program_id(1) == 0)
    def _():
        uext_ref[0:SUBLANES, :] = jnp.zeros((SUBLANES, MLSTM_PAD), f32)
        c_ref[...] = jnp.zeros_like(c_ref)
        m_ref[...] = jnp.zeros_like(m_ref)

    u = mu_ref[...]
    uext_ref[SUBLANES:SUBLANES + L, :] = u
    acc = jnp.broadcast_to(cb_ref[...], (L, MLSTM_PAD))
    for k in range(CONV_WIDTH):
        start = SUBLANES - (CONV_WIDTH - 1) + k
        acc = acc + cw_ref[k:k + 1, :] * uext_ref[start:start + L, :]
    c = acc * jax.nn.sigmoid(acc)
    uext_ref[0:SUBLANES, :] = u[L - SUBLANES:L, :]

    cb = c.astype(bf16)
    q = _dot(cb, wq_ref[...]).astype(bf16)
    kf = _dot(cb, wk_ref[...])
    vf = _dot(u.astype(bf16), wv_ref[...])

    gcol = gc_ref[...] + gbr_ref[...]
    grow = gr_ref[...] + gbc_ref[...]
    lf_col = _log_sigmoid(gcol)
    lf_row = _log_sigmoid(grow)
    ri = lax.broadcasted_iota(i32, (L, L), 0)
    ci = lax.broadcasted_iota(i32, (L, L), 1)
    causal = ci <= ri
    b_col = _dot_exact(causal.astype(f32), lf_col)
    b_row = _dot_exact(lf_row, (ri <= ci).astype(f32))
    lane = lax.broadcasted_iota(i32, (L, LANES), 1)
    o_gate = jax.nn.sigmoid(mo_ref[...])

    for h in range(H):
        sl = slice(h * LANES, (h + 1) * LANES)
        bc = b_col[:, H + h:H + h + 1]
        br = b_row[H + h:H + h + 1, :]
        lic = gcol[:, h:h + 1]
        lir = grow[h:h + 1, :]
        g = bc[L - 1:L, :]
        m_prev = m_ref[h][0:1, 0:1]
        c_prev = c_ref[h]

        d = jnp.where(causal, bc - br + lir, -jnp.inf)
        m_inter = bc + m_prev
        m_t = jnp.maximum(m_inter, jnp.max(d, axis=-1, keepdims=True))
        qh = q[:, sl]
        kh = kf[:, sl]
        v_aug = jnp.where(lane == HEAD_DIM, 1.0, vf[:, sl]).astype(bf16)
        w_intra = jnp.exp(d - m_t) * _dot_nt(qh, kh.astype(bf16))
        inter = jnp.exp(m_inter - m_t)
        nd = inter * _dot(qh, c_prev.astype(bf16)) + _dot(w_intra.astype(bf16), v_aug)
        den = nd[:, HEAD_DIM:HEAD_DIM + 1]
        hh = nd / jnp.maximum(jnp.abs(den), jnp.exp(-m_t))
        hh = jnp.where(lane < HEAD_DIM, hh, 0.0)
        hn = hh * lax.rsqrt(jnp.sum(hh * hh, axis=-1, keepdims=True) * (1.0 / HEAD_DIM) + RMS_EPS)
        y = o_gate[:, sl] * (hn * ng_ref[:, sl] + sk_ref[:, sl] * c[:, sl])
        y_ref[:, sl] = y.astype(y_ref.dtype)

        a = g - bc + lic
        m_new = jnp.maximum(g + m_prev, jnp.max(a, axis=0, keepdims=True))
        decay = jnp.exp(g + m_prev - m_new)
        kw = (kh * jnp.exp(a - m_new)).astype(bf16)
        c_ref[h] = decay * c_prev + _dot_tn(kw, v_aug)
        m_ref[h] = jnp.broadcast_to(m_new, (SUBLANES, LANES))


def _mlstm(mu, mo, gates, p, batch):
    t = mu.shape[0]
    s = t // batch
    L = MLSTM_CHUNK
    nc = s // L
    g_row = jnp.transpose(gates.reshape(batch, s, LANES)[:, :, :SUBLANES], (0, 2, 1))
    tok = lambda b, c: (b * nc + c, 0)
    w_spec = _full((MLSTM_PAD, MLSTM_PAD))
    v_spec = _full((1, MLSTM_PAD))
    return pl.pallas_call(
        _mlstm_kernel,
        grid=(batch, nc),
        in_specs=[pl.BlockSpec((L, MLSTM_PAD), tok), pl.BlockSpec((L, MLSTM_PAD), tok),
                  pl.BlockSpec((L, LANES), tok),
                  pl.BlockSpec((None, SUBLANES, L), lambda b, c: (b, 0, c)),
                  _full((CONV_WIDTH, MLSTM_PAD)), v_spec, w_spec, w_spec, w_spec,
                  _full((1, LANES)), _full((SUBLANES, 1)), v_spec, v_spec],
        out_specs=pl.BlockSpec((L, MLSTM_PAD), tok),
        out_shape=jax.ShapeDtypeStruct((t, MLSTM_PAD), bf16),
        scratch_shapes=[pltpu.VMEM((L + SUBLANES, MLSTM_PAD), f32),
                        pltpu.VMEM((MLSTM_HEADS, LANES, LANES), f32),
                        pltpu.VMEM((MLSTM_HEADS, SUBLANES, LANES), f32)],
        compiler_params=_cparams("arbitrary", "arbitrary"),
        name="mlstm",
    )(mu, mo, gates, g_row, p["conv_w"], p["conv_b"], p["wq"], p["wk"], p["wv"],
      p["gb_row"], p["gb_col"], p["norm"], p["skip"])


def _s5_param_kernel(are_ref, aim_ref, ldt_ref, bre_ref, bim_ref, pwr_ref, pwi_ref, bbr_ref, bbi_ref):
    a_re = are_ref[...]
    a_im = aim_ref[...]
    dt = jnp.exp(ldt_ref[...])
    mag = jnp.exp(a_re * dt)
    ab_re = mag * jnp.cos(a_im * dt)
    ab_im = mag * jnp.sin(a_im * dt)
    den = a_re * a_re + a_im * a_im
    xr, yi = ab_re - 1.0, ab_im
    coef_re = (xr * a_re + yi * a_im) / den
    coef_im = (yi * a_re - xr * a_im) / den
    b_re = bre_ref[...]
    b_im = bim_ref[...]
    bbr_ref[...] = coef_re[:, None, :] * b_re - coef_im[:, None, :] * b_im
    bbi_ref[...] = coef_re[:, None, :] * b_im + coef_im[:, None, :] * b_re
    pr, pi = ab_re, ab_im
    pwr_ref[0] = pr
    pwi_ref[0] = pi
    for k in range(1, SUBLANES):
        pr, pi = pr * ab_re - pi * ab_im, pr * ab_im + pi * ab_re
        pwr_ref[k] = pr
        pwi_ref[k] = pi


def _s5_params(a_re, a_im, log_dt, b_re, b_im):
    G, P, Hc = S5_GROUPS, S5_STATE, S5_GROUP_CH
    bt_re = jnp.transpose(b_re, (0, 2, 1))
    bt_im = jnp.transpose(b_im, (0, 2, 1))
    return pl.pallas_call(
        _s5_param_kernel,
        out_shape=[jax.ShapeDtypeStruct((SUBLANES, G, P), f32), jax.ShapeDtypeStruct((SUBLANES, G, P), f32),
                   jax.ShapeDtypeStruct((G, Hc, P), f32), jax.ShapeDtypeStruct((G, Hc, P), f32)],
        name="s5_params",
    )(a_re, a_im, log_dt.reshape(G, 1), bt_re, bt_im)


def _gelu_tanh(x):
    return 0.5 * x * (1.0 + jnp.tanh(math.sqrt(2.0 / math.pi) * (x + 0.044715 * (x * x * x))))


def _s5_kernel(u_ref, wb_ref, pwr_ref, pwi_ref, wc_ref, dsk_ref, wg_ref, bg_ref, ng_ref, y_ref,
               sr_ref, si_ref, cr_ref, ci_ref):
    TT = u_ref.shape[0]
    NL = S5_LANES
    ng8 = TT // SUBLANES

    @pl.when(pl.program_id(1) == 0)
    def _():
        cr_ref[...] = jnp.zeros_like(cr_ref)
        ci_ref[...] = jnp.zeros_like(ci_ref)

    u = u_ref[...]
    bu = _dot(u.astype(bf16), wb_ref[...])
    xr = bu[:, :NL].reshape(ng8, SUBLANES, NL)
    xi = bu[:, NL:].reshape(ng8, SUBLANES, NL)
    row = lax.broadcasted_iota(i32, (1, SUBLANES, NL), 1)
    for sh in (1, 2, 4):
        ar = pwr_ref[sh - 1:sh, :][None]
        ai = pwi_ref[sh - 1:sh, :][None]
        tr = pltpu.roll(xr, sh, 1)
        ti = pltpu.roll(xi, sh, 1)
        keep = row >= sh
        xr, xi = (xr + jnp.where(keep, ar * tr - ai * ti, 0.0),
                  xi + jnp.where(keep, ar * ti + ai * tr, 0.0))
    sr_ref[...] = xr.reshape(TT, NL)
    si_ref[...] = xi.reshape(TT, NL)

    pr = pwr_ref[...]
    pi = pwi_ref[...]

    def group(gidx, carry):
        c_re, c_im = carry
        r0 = pl.multiple_of(gidx * SUBLANES, SUBLANES)
        t_re = sr_ref[pl.ds(r0, SUBLANES), :] + pr * c_re - pi * c_im
        t_im = si_ref[pl.ds(r0, SUBLANES), :] + pr * c_im + pi * c_re
        sr_ref[pl.ds(r0, SUBLANES), :] = t_re
        si_ref[pl.ds(r0, SUBLANES), :] = t_im
        return t_re[SUBLANES - 1:SUBLANES, :], t_im[SUBLANES - 1:SUBLANES, :]

    c_re, c_im = lax.fori_loop(0, ng8, group, (cr_ref[...], ci_ref[...]))
    cr_ref[...] = c_re
    ci_ref[...] = c_im

    y = (_dot(sr_ref[...].astype(bf16), wc_ref[0:NL, :]) + _dot(si_ref[...].astype(bf16), wc_ref[NL:2 * NL, :])
         + dsk_ref[...] * u)
    z = _gelu_tanh(y)
    out = z * jax.nn.sigmoid(_dot(z.astype(bf16), wg_ref[...]) + bg_ref[...])
    y_ref[...] = _rms(out, ng_ref[...]).astype(y_ref.dtype)


def _s5(su, p, batch):
    t = su.shape[0]
    s = t // batch
    tt = min(S5_BLOCK, s)
    nb = s // tt
    W, NL = S5_WIDTH, S5_LANES
    tok = lambda b, c: (b * nb + c, 0)
    return pl.pallas_call(
        _s5_kernel,
        grid=(batch, nb),
        in_specs=[pl.BlockSpec((tt, W), tok), _full((W, 2 * NL)), _full((SUBLANES, NL)), _full((SUBLANES, NL)),
                  _full((2 * NL, W)), _full((1, W)), _full((W, W)), _full((1, W)), _full((1, W))],
        out_specs=pl.BlockSpec((tt, W), tok),
        out_shape=jax.ShapeDtypeStruct((t, W), bf16),
        scratch_shapes=[pltpu.VMEM((tt, NL), f32), pltpu.VMEM((tt, NL), f32),
                        pltpu.VMEM((1, NL), f32), pltpu.VMEM((1, NL), f32)],
        compiler_params=_cparams("arbitrary", "arbitrary"),
        name="s5",
    )(su, p["wb"], p["pw_re"], p["pw_im"], p["wc"], p["d"], p["w_glu"], p["b_glu"], p["norm"])


def _t5_bucket(dist):
    exact = dist < REL_MAX_EXACT
    large = REL_MAX_EXACT + (np.log(np.maximum(dist, 1) / REL_MAX_EXACT)
                             / np.log(REL_MAX_DIST / REL_MAX_EXACT)
                             * (REL_BUCKETS - REL_MAX_EXACT)).astype(np.int32)
    large = np.minimum(large, REL_BUCKETS - 1)
    return np.where(exact, dist, large).astype(np.int32)


def _band_offsets():
    i = np.arange(ATTN_BLOCK)[:, None]
    j = np.arange(2 * ATTN_BLOCK)[None, :]
    return ATTN_BLOCK + i - j


def _attn_bias(rel_table, dil):
    bucket = _t5_bucket(np.clip(_band_offsets(), 0, None) * dil)
    return jnp.transpose(rel_table.astype(f32)[bucket], (2, 0, 1))


def _attn_kernel(q_ref, kc_ref, kp_ref, vc_ref, vp_ref, bias_ref, o_ref, lse_ref, kcat_ref, vcat_ref, *,
                 n_back):
    RB = q_ref.shape[0]
    BLK = ATTN_BLOCK
    n = pl.program_id(2)
    kcat_ref[0:BLK, :] = kp_ref[...]
    kcat_ref[BLK:BLK + RB, :] = kc_ref[...]
    vcat_ref[0:BLK, :] = vp_ref[...]
    vcat_ref[BLK:BLK + RB, :] = vc_ref[...]

    ri = lax.broadcasted_iota(i32, (BLK, 2 * BLK), 0)
    ci = lax.broadcasted_iota(i32, (BLK, 2 * BLK), 1)
    off = BLK + ri - ci
    band = (off >= 0) & (off <= n_back)
    lane_q = lax.broadcasted_iota(i32, (BLK, LANES), 1)
    lane_kv = lax.broadcasted_iota(i32, (2 * BLK, LANES), 1)

    def sub_block(j, _):
        r0 = pl.multiple_of(j * BLK, BLK)
        has_prev = (n > 0) | (j > 0)
        valid = band & (has_prev | (ci >= BLK))
        for hp in range(ATTN_HEADS // 2):
            cs = slice(hp * LANES, (hp + 1) * LANES)
            qp = q_ref[pl.ds(r0, BLK), cs]
            kp = kcat_ref[pl.ds(r0, 2 * BLK), cs]
            vp = vcat_ref[pl.ds(r0, 2 * BLK), cs]
            o_pair = None
            lse_pair = None
            for e in range(2):
                in_head_q = (lane_q >= HEAD_DIM) == (e == 1)
                in_head_kv = (lane_kv >= HEAD_DIM) == (e == 1)
                qm = jnp.where(in_head_q, qp, jnp.zeros_like(qp))
                s = _dot_nt(qm, kp) + bias_ref[2 * hp + e]
                s = jnp.where(valid, s, NEG_INF)
                m = jnp.max(s, axis=-1, keepdims=True)
                pexp = jnp.exp(s - m)
                l = jnp.sum(pexp, axis=-1, keepdims=True)
                vm = jnp.where(in_head_kv, vp, jnp.zeros_like(vp))
                o_e = _dot((pexp / l).astype(bf16), vm)
                lse_e = jnp.broadcast_to(m + jnp.log(l), (BLK, LANES))
                o_pair = o_e if o_pair is None else o_pair + o_e
                lse_pair = lse_e if lse_pair is None else jnp.where(in_head_q, lse_e, lse_pair)
            o_ref[pl.ds(r0, BLK), cs] = o_pair
            lse_ref[pl.ds(r0, BLK), cs] = lse_pair
        return 0

    lax.fori_loop(0, RB // BLK, sub_block, 0)


def _dilated_attn(q, k, v, bias, window, dil, batch):
    w = ATTN_WIDTH
    n = q.shape[0] // batch
    rb = min(TOKEN_BLOCK, n)
    nblk = n // rb
    per = rb // ATTN_BLOCK
    view = lambda a: a.reshape(batch, n, dil * w)
    cur = pl.BlockSpec((None, rb, w), lambda b, r, i: (b, i, r))
    prev = pl.BlockSpec((None, ATTN_BLOCK, w), lambda b, r, i: (b, jnp.maximum(i * per - 1, 0), r))
    out = pl.BlockSpec((None, rb, w), lambda b, r, i: (b, i, r))
    o, lse = pl.pallas_call(
        functools.partial(_attn_kernel, n_back=window // dil),
        grid=(batch, dil, nblk),
        in_specs=[cur, cur, prev, cur, prev, _full(bias.shape)],
        out_specs=[out, out],
        out_shape=[jax.ShapeDtypeStruct((batch, n, dil * w), f32)] * 2,
        scratch_shapes=[pltpu.VMEM((rb + ATTN_BLOCK, w), bf16), pltpu.VMEM((rb + ATTN_BLOCK, w), bf16)],
        compiler_params=_cparams("parallel", "parallel", "arbitrary"),
        name=f"dilated_attn_{dil}",
    )(view(q), view(k), view(k), view(v), view(v), bias)
    return o.reshape(batch * n, dil * w), lse.reshape(batch * n, dil * w)


def _token_order(ref, scr_ref, slot, dil, tm):
    if dil == 1:
        return ref[...]
    for r in range(dil):
        for c in range(ATTN_TILES):
            col = r * ATTN_WIDTH + c * LANES
            scr_ref[slot, c, pl.ds(r, tm // dil, stride=dil), :] = ref[:, col:col + LANES]
    return jnp.concatenate([scr_ref[slot, c] for c in range(ATTN_TILES)], axis=1)


def _out_proj_kernel(x_ref, ym_ref, ys_ref, o1_ref, o2_ref, o3_ref, l1_ref, l2_ref, l3_ref, ag_ref,
                     wm_ref, ws_ref, wa_ref, out_ref, scr_ref):
    tm = x_ref.shape[0]
    o_refs, l_refs = (o1_ref, o2_ref, o3_ref), (l1_ref, l2_ref, l3_ref)
    o = [_token_order(r, scr_ref, 2 * i, dil, tm) for i, (r, dil) in enumerate(zip(o_refs, DILATIONS))]
    lse = [_token_order(r, scr_ref, 2 * i + 1, dil, tm) for i, (r, dil) in enumerate(zip(l_refs, DILATIONS))]
    mx = jnp.maximum(jnp.maximum(lse[0], lse[1]), lse[2])
    e = [jnp.exp(l - mx) for l in lse]
    tot = e[0] + e[1] + e[2]
    ya = (e[0] / tot) * o[0] + (e[1] / tot) * o[1] + (e[2] / tot) * o[2]
    ya = _rms(ya, ag_ref[...]).astype(bf16)
    out_ref[...] = (x_ref[...] + _dot(ym_ref[...], wm_ref[...]) + _dot(ys_ref[...], ws_ref[...])
                    + _dot(ya, wa_ref[...]))


def _out_proj(x2, ym, ys, attn, a_gain, wm, ws, wa):
    t, d = x2.shape
    tm = min(TOKEN_BLOCK, t)
    row = lambda w: pl.BlockSpec((tm, w), lambda i: (i, 0))
    dilated = [pl.BlockSpec((tm // dil, dil * ATTN_WIDTH), lambda i: (i, 0)) for dil in DILATIONS]
    (o1, l1), (o2, l2), (o3, l3) = attn
    return pl.pallas_call(
        _out_proj_kernel,
        grid=(t // tm,),
        in_specs=[row(d), row(MLSTM_PAD), row(S5_WIDTH)] + dilated + dilated
                 + [_full((1, ATTN_WIDTH)), _full(wm.shape), _full(ws.shape), _full(wa.shape)],
        out_specs=row(d),
        out_shape=jax.ShapeDtypeStruct((t, d), f32),
        scratch_shapes=[pltpu.VMEM((2 * len(DILATIONS), ATTN_TILES, tm, LANES), f32)],
        compiler_params=_cparams("parallel"),
        name="out_proj",
    )(x2, ym, ys, o1, o2, o3, l1, l2, l3, a_gain, wm, ws, wa)


def _mem_kv_kernel(mem_ref, g_ref, w_ref, kv_ref):
    kv_ref[...] = _dot(_rms(mem_ref[...], g_ref[...]).astype(bf16), w_ref[...]).astype(kv_ref.dtype)


def _mem_kv(mem2, gain, w_kv):
    r, d = mem2.shape
    return pl.pallas_call(
        _mem_kv_kernel,
        out_shape=jax.ShapeDtypeStruct((r, 2 * d), bf16),
        compiler_params=pltpu.CompilerParams(vmem_limit_bytes=VMEM_LIMIT),
        name="mem_kv",
    )(mem2, gain, w_kv)


def _xattn_kernel(x_ref, g_ref, wq_ref, kv_ref, wo_ref, out_ref):
    x = x_ref[...]
    d = x.shape[1]
    hd = d // MEM_HEADS
    q = _dot(_rms(x, g_ref[...]).astype(bf16), wq_ref[...]).astype(bf16)
    heads = []
    for h in range(MEM_HEADS):
        kh = kv_ref[:, h * hd:(h + 1) * hd]
        vh = kv_ref[:, d + h * hd:d + (h + 1) * hd]
        s = _dot_nt(q[:, h * hd:(h + 1) * hd], kh) * (hd ** -0.5)
        m = jnp.max(s, axis=-1, keepdims=True)
        p = jnp.exp(s - m)
        p = p / jnp.sum(p, axis=-1, keepdims=True)
        heads.append(_dot(p.astype(bf16), vh).astype(bf16))
    o = jnp.concatenate(heads, axis=-1)
    out_ref[...] = x + _dot(o, wo_ref[...])


def _xattn(x2, gain, w_q, kv, w_o, batch):
    t, d = x2.shape
    s = t // batch
    m = kv.shape[0] // batch
    tm = min(TOKEN_BLOCK, s)
    nb = s // tm
    tok = lambda b, i: (b * nb + i, 0)
    return pl.pallas_call(
        _xattn_kernel,
        grid=(batch, nb),
        in_specs=[pl.BlockSpec((tm, d), tok), _full((1, d)), _full((d, d)),
                  pl.BlockSpec((m, 2 * d), lambda b, i: (b, 0)), _full((d, d))],
        out_specs=pl.BlockSpec((tm, d), tok),
        out_shape=jax.ShapeDtypeStruct((t, d), f32),
        compiler_params=_cparams("parallel", "parallel"),
        name="xattn",
    )(x2, gain, w_q, kv, w_o)


def _first_argmax(x, lane):
    m = jnp.max(x, axis=-1, keepdims=True)
    idx = jnp.min(jnp.where(x == m, lane, LANES), axis=-1, keepdims=True)
    return m, idx


def _router_kernel(x_ref, g_ref, wr_ref, br_ref, xn_ref, sel_ref, gate_ref, cnt_ref, run_ref):
    TM = x_ref.shape[0]
    G, EPG, E = MOE_GROUPS, EXPERTS_PER_GROUP, N_EXPERTS

    @pl.when(pl.program_id(0) == 0)
    def _():
        run_ref[...] = jnp.zeros_like(run_ref)

    xn = _rms(x_ref[...], g_ref[...])
    xn_ref[...] = xn
    logits = _dot_exact(xn, wr_ref[...]) + br_ref[...]
    lane = lax.broadcasted_iota(i32, (TM, LANES), 1)
    is_group = (lane >= E) & (lane < E + G)
    gl = jnp.where(is_group, logits, -jnp.inf)
    gmax, gidx = _first_argmax(gl, lane)
    g_p = 1.0 / jnp.sum(jnp.exp(gl - gmax), axis=-1, keepdims=True)
    g_i = gidx - E
    in_group = (lane < E) & ((lane >> int(math.log2(EPG))) == g_i)
    el = jnp.where(in_group, logits, -jnp.inf)
    m1, i1 = _first_argmax(el, lane)
    m2, i2 = _first_argmax(jnp.where(lane == i1, -jnp.inf, el), lane)
    p2 = jnp.exp(m2 - m1)
    gate1 = g_p / (1.0 + p2)
    gate2 = g_p * p2 / (1.0 + p2)

    onehot = ((lane == i1) | (lane == i2)).astype(bf16)
    r = lax.broadcasted_iota(i32, (TM, TM), 0)
    c = lax.broadcasted_iota(i32, (TM, TM), 1)
    before = _dot((c < r).astype(bf16), onehot) + run_ref[0:1, :]
    rank1 = jnp.sum(jnp.where(lane == i1, before, 0.0), axis=-1, keepdims=True)
    rank2 = jnp.sum(jnp.where(lane == i2, before, 0.0), axis=-1, keepdims=True)
    run_ref[...] = run_ref[...] + jnp.sum(onehot.astype(f32), axis=0, keepdims=True)
    cnt_ref[...] = run_ref[...]

    sel = jnp.where(lane == 0, i1, jnp.where(lane == 1, i2, 0))
    sel = jnp.where(lane == 2, rank1.astype(i32), jnp.where(lane == 3, rank2.astype(i32), sel))
    sel_ref[...] = sel
    gate_ref[...] = jnp.where(lane == 0, gate1, jnp.where(lane == 1, gate2, 0.0))


def _router(x2, gain, w_r, b_r):
    t, d = x2.shape
    tm = min(TOKEN_BLOCK, t)
    row = lambda w: pl.BlockSpec((tm, w), lambda i: (i, 0))
    return pl.pallas_call(
        _router_kernel,
        grid=(t // tm,),
        in_specs=[row(d), _full((1, d)), _full((d, LANES)), _full((1, LANES))],
        out_specs=[row(d), row(LANES), row(LANES), _full((SUBLANES, LANES))],
        out_shape=[jax.ShapeDtypeStruct((t, d), f32), jax.ShapeDtypeStruct((t, LANES), i32),
                   jax.ShapeDtypeStruct((t, LANES), f32), jax.ShapeDtypeStruct((SUBLANES, LANES), f32)],
        scratch_shapes=[pltpu.VMEM((SUBLANES, LANES), f32)],
        compiler_params=_cparams("arbitrary"),
        name="moe_router",
    )(x2, gain, w_r, b_r)


def _row_copy(src_ref, src_row, dst_ref, dst_row, sem):
    return pltpu.make_async_copy(src_ref.at[pl.ds(src_row, 1), :], dst_ref.at[pl.ds(dst_row, 1), :], sem)


def _dispatch_kernel(dest_ref, xn_ref, rows_in_ref, rows_ref, sem):
    del rows_in_ref
    TM = xn_ref.shape[0]
    base = pl.program_id(0) * (2 * TM)

    def issue(a, _):
        _row_copy(xn_ref, a // 2, rows_ref, dest_ref[base + a], sem).start()
        return 0

    lax.fori_loop(0, 2 * TM, issue, 0)

    def drain(a, _):
        _row_copy(xn_ref, a // 2, rows_ref, dest_ref[base + a], sem).wait()
        return 0

    lax.fori_loop(0, 2 * TM, drain, 0)


def _dispatch(dest, xn, n_rows):
    t, d = xn.shape
    tm = min(TOKEN_BLOCK, t)
    rows0 = jnp.zeros((n_rows, d), f32)
    return pl.pallas_call(
        _dispatch_kernel,
        grid_spec=pltpu.PrefetchScalarGridSpec(
            num_scalar_prefetch=1,
            grid=(t // tm,),
            in_specs=[pl.BlockSpec((tm, d), lambda i, dest: (i, 0)), pl.BlockSpec(memory_space=pl.ANY)],
            out_specs=pl.BlockSpec(memory_space=pl.ANY),
            scratch_shapes=[pltpu.SemaphoreType.DMA(())],
        ),
        out_shape=jax.ShapeDtypeStruct((n_rows, d), f32),
        input_output_aliases={2: 0},
        compiler_params=_cparams("arbitrary"),
        name="moe_dispatch",
    )(dest, xn, rows0)


def _expert_kernel(be_ref, nu_ref, x_ref, wgu_ref, wd_ref, y_ref):
    i = pl.program_id(0)
    de = wd_ref.shape[0]

    @pl.when(i < nu_ref[0])
    def _():
        ab = _dot(x_ref[...].astype(bf16), wgu_ref[...])
        a, b = ab[:, :de], ab[:, de:]
        hid = (a * jax.nn.sigmoid(a) * b).astype(bf16)
        y_ref[...] = _dot(hid, wd_ref[...])

    @pl.when(i >= nu_ref[0])
    def _():
        y_ref[...] = jnp.zeros_like(y_ref)


def _experts(block_e, n_used, rows, w_gu, w_down):
    n_rows, d = rows.shape
    nblk = n_rows // MOE_ROWS
    de = w_down.shape[1]
    return pl.pallas_call(
        _expert_kernel,
        grid_spec=pltpu.PrefetchScalarGridSpec(
            num_scalar_prefetch=2,
            grid=(nblk,),
            in_specs=[pl.BlockSpec((MOE_ROWS, d), lambda i, be, nu: (i, 0)),
                      pl.BlockSpec((None, d, 2 * de), lambda i, be, nu: (be[i], 0, 0)),
                      pl.BlockSpec((None, de, d), lambda i, be, nu: (be[i], 0, 0))],
            out_specs=pl.BlockSpec((MOE_ROWS, d), lambda i, be, nu: (i, 0)),
        ),
        out_shape=jax.ShapeDtypeStruct((n_rows, d), f32),
        compiler_params=_cparams("arbitrary"),
        name="moe_experts",
    )(block_e, n_used, rows, w_gu, w_down)


def _combine_kernel(dest_ref, x_ref, gate_ref, fg_ref, y_hbm, out_ref, buf0, buf1, sem, *, final_norm):
    TM = x_ref.shape[0]
    base = pl.program_id(0) * (2 * TM)

    def issue(tk, _):
        _row_copy(y_hbm, dest_ref[base + 2 * tk], buf0, tk, sem).start()
        _row_copy(y_hbm, dest_ref[base + 2 * tk + 1], buf1, tk, sem).start()
        return 0

    lax.fori_loop(0, TM, issue, 0)

    def drain(tk, _):
        _row_copy(y_hbm, dest_ref[base + 2 * tk], buf0, tk, sem).wait()
        _row_copy(y_hbm, dest_ref[base + 2 * tk + 1], buf1, tk, sem).wait()
        return 0

    lax.fori_loop(0, TM, drain, 0)
    gates = gate_ref[...]
    out = x_ref[...] + gates[:, 0:1] * buf0[...] + gates[:, 1:2] * buf1[...]
    if final_norm:
        out = _rms(out, fg_ref[...])
    out_ref[...] = out


def _combine(dest, x2, gates, y_rows, final_gain, final_norm):
    t, d = x2.shape
    tm = min(TOKEN_BLOCK, t)
    return pl.pallas_call(
        functools.partial(_combine_kernel, final_norm=final_norm),
        grid_spec=pltpu.PrefetchScalarGridSpec(
            num_scalar_prefetch=1,
            grid=(t // tm,),
            in_specs=[pl.BlockSpec((tm, d), lambda i, dest: (i, 0)),
                      pl.BlockSpec((tm, LANES), lambda i, dest: (i, 0)),
                      pl.BlockSpec((1, d), lambda i, dest: (0, 0)),
                      pl.BlockSpec(memory_space=pl.ANY)],
            out_specs=pl.BlockSpec((tm, d), lambda i, dest: (i, 0)),
            scratch_shapes=[pltpu.VMEM((tm, d), f32), pltpu.VMEM((tm, d), f32), pltpu.SemaphoreType.DMA(())],
        ),
        out_shape=jax.ShapeDtypeStruct((t, d), f32),
        compiler_params=_cparams("arbitrary"),
        name="moe_combine",
    )(dest, x2, gates, final_gain, y_rows)


def _moe(x2, gain, w_r, b_r, w_gu, w_down, final_gain, final_norm):
    t, d = x2.shape
    xn, sel, gates, counts = _router(x2, gain, w_r, b_r)
    counts = counts[0, :N_EXPERTS].astype(i32)
    padded = (counts + MOE_ROWS - 1) // MOE_ROWS * MOE_ROWS
    p_ends = jnp.cumsum(padded)
    p_starts = p_ends - padded
    n_blocks = (2 * t) // MOE_ROWS + N_EXPERTS
    dest = (p_starts[sel[:, 0:2]] + sel[:, 2:4]).reshape(-1).astype(i32)
    block_start = jnp.arange(n_blocks, dtype=i32) * MOE_ROWS
    block_e = jnp.minimum(jnp.sum((p_ends[None, :] <= block_start[:, None]).astype(i32), axis=1), N_EXPERTS - 1)
    n_used = (p_ends[-1:] // MOE_ROWS).astype(i32)
    rows = _dispatch(dest, xn, n_blocks * MOE_ROWS)
    y_rows = _experts(block_e, n_used, rows, w_gu, w_down)
    return _combine(dest, x2, gates, y_rows, final_gain, final_norm)


def _layer_params(l, a):
    H, W = MLSTM_HEADS, MLSTM_WIDTH
    w_in = a["w_in"][l]
    sizes = [W, W, H, H, S5_WIDTH, ATTN_WIDTH, ATTN_WIDTH, ATTN_WIDTH]
    offs = np.concatenate([[0], np.cumsum(sizes)])
    seg = lambda i: w_in[:, offs[i]:offs[i + 1]]
    gates_w = jnp.pad(jnp.concatenate([seg(2), seg(3)], axis=1), ((0, 0), (0, LANES - 2 * H)))
    w_cat = jnp.concatenate([_pad_heads(seg(0), 1), _pad_heads(seg(1), 1), gates_w, seg(4),
                             seg(5) * (HEAD_DIM ** -0.5), seg(6), seg(7)], axis=1).astype(bf16)
    row = lambda v: v.reshape(1, -1)
    gb = a["mlstm_gate_bias"][l]
    mlstm = dict(
        conv_w=_pad_heads(a["mlstm_conv_w"][l], 1), conv_b=row(_pad_heads(a["mlstm_conv_b"][l], 0)),
        wq=_block_diag_heads(a["mlstm_wq"][l] * (HEAD_DIM ** -0.5)).astype(bf16),
        wk=_block_diag_heads(a["mlstm_wk"][l]).astype(bf16),
        wv=_block_diag_heads(a["mlstm_wv"][l]).astype(bf16),
        gb_row=jnp.pad(gb, (0, LANES - 2 * H)).reshape(1, LANES), gb_col=gb.reshape(2 * H, 1),
        norm=row(_pad_heads(a["mlstm_norm"][l], 0)), skip=row(_pad_heads(a["mlstm_skip"][l], 0)))

    G, P, Hc = S5_GROUPS, S5_STATE, S5_GROUP_CH
    pw_re, pw_im, bb_re, bb_im = _s5_params(a["s5_a_re"][l], a["s5_a_im"][l], a["s5_log_dt"][l],
                                            a["s5_b_re"][l], a["s5_b_im"][l])
    eye = jnp.eye(G, dtype=f32)
    bd_in = lambda m: (m[:, :, None, :] * eye[:, None, :, None]).reshape(G * Hc, G * P)
    bd_out = lambda m: (jnp.transpose(m, (0, 2, 1))[:, :, None, :] * eye[:, None, :, None]).reshape(G * P, G * Hc)
    s5 = dict(
        wb=jnp.concatenate([bd_in(bb_re), bd_in(bb_im)], axis=1).astype(bf16),
        wc=jnp.concatenate([bd_out(a["s5_c_re"][l]), -bd_out(a["s5_c_im"][l])], axis=0).astype(bf16),
        pw_re=pw_re.reshape(SUBLANES, G * P), pw_im=pw_im.reshape(SUBLANES, G * P),
        d=a["s5_d"][l].reshape(1, S5_WIDTH), w_glu=a["s5_w_glu"][l].astype(bf16),
        b_glu=row(a["s5_b_glu"][l]), norm=row(a["s5_out_norm"][l]))

    w_out = a["w_out"][l]
    w_r = jnp.pad(jnp.concatenate([a["router_w_expert"][l], a["router_w_group"][l]], axis=1),
                  ((0, 0), (0, LANES - N_EXPERTS - MOE_GROUPS)))
    b_r = jnp.pad(jnp.concatenate([a["router_b_expert"][l], a["router_b_group"][l]]),
                  (0, LANES - N_EXPERTS - MOE_GROUPS)).reshape(1, LANES)
    return dict(
        norm_mix=row(a["norm_mix"][l]), w_cat=w_cat, mlstm=mlstm, s5=s5,
        attn_norm=row(a["attn_out_norm"][l]),
        wo_m=_pad_heads(w_out[:W], 0).astype(bf16), wo_s=w_out[W:W + S5_WIDTH].astype(bf16),
        wo_a=w_out[W + S5_WIDTH:].astype(bf16),
        norm_xattn=row(a["norm_xattn"][l]), norm_mem=row(a["norm_mem"][l]),
        xq=a["xattn_w_q"][l].astype(bf16), xkv=a["xattn_w_kv"][l].astype(bf16), xo=a["xattn_w_o"][l].astype(bf16),
        norm_ffn=row(a["norm_ffn"][l]), w_r=w_r, b_r=b_r,
        w_gu=a["expert_w_gu"][l].astype(bf16), w_down=a["expert_w_down"][l].astype(bf16))


def _forward(a):
    x = a["x"]
    batch, seq, d = x.shape
    depth = a["w_in"].shape[0]
    x2 = x.reshape(batch * seq, d)
    mem2 = a["mem"].reshape(-1, d)
    biases = [_attn_bias(a["rel_bias"], dil) for _, dil in DILATED_PATTERNS]
    final_gain = a["final_norm"].reshape(1, d)
    for l in range(depth):
        p = _layer_params(l, a)
        mu, mo, gates, su, *qkv = _in_proj(x2, p["norm_mix"], p["w_cat"])
        ym = _mlstm(mu, mo, gates, p["mlstm"], batch)
        ys = _s5(su, p["s5"], batch)
        nd = len(DILATIONS)
        attn = [_dilated_attn(qkv[i], qkv[nd + i], qkv[2 * nd + i], bias, window, dil, batch)
                for i, (bias, (window, dil)) in enumerate(zip(biases, DILATED_PATTERNS))]
        x2 = _out_proj(x2, ym, ys, attn, p["attn_norm"], p["wo_m"], p["wo_s"], p["wo_a"])
        kv = _mem_kv(mem2, p["norm_mem"], p["xkv"])
        x2 = _xattn(x2, p["norm_xattn"], p["xq"], kv, p["xo"], batch)
        x2 = _moe(x2, p["norm_ffn"], p["w_r"], p["b_r"], p["w_gu"], p["w_down"], final_gain,
                  final_norm=(l == depth - 1))
    return x2.reshape(batch, seq, d)


def kernel(x, mem, rel_bias, norm_mix, w_in, mlstm_conv_w, mlstm_conv_b, mlstm_wq, mlstm_wk, mlstm_wv,
           mlstm_gate_bias, mlstm_norm, mlstm_skip, s5_a_re, s5_a_im, s5_log_dt, s5_b_re, s5_b_im, s5_c_re,
           s5_c_im, s5_d, s5_w_glu, s5_b_glu, s5_out_norm, attn_out_norm, w_out, norm_xattn, norm_mem,
           xattn_w_q, xattn_w_kv, xattn_w_o, norm_ffn, router_w_group, router_b_group, router_w_expert,
           router_b_expert, expert_w_gu, expert_w_down, final_norm):
    return _forward(dict(locals()))
```

```python
import functools
import math

import numpy as np
import jax
import jax.numpy as jnp
from jax import lax
from jax.experimental import pallas as pl
from jax.experimental.pallas import tpu as pltpu

f32 = jnp.float32
bf16 = jnp.bfloat16
i32 = jnp.int32

HEAD_DIM = 64
MLSTM_HEADS = 4
MLSTM_WIDTH = MLSTM_HEADS * HEAD_DIM
CONV_WIDTH = 4
S5_GROUPS = 16
S5_GROUP_CH = 16
S5_STATE = 64
S5_WIDTH = S5_GROUPS * S5_GROUP_CH
S5_LANES = S5_GROUPS * S5_STATE
ATTN_HEADS = 8
ATTN_WIDTH = ATTN_HEADS * HEAD_DIM
ATTN_BLOCK = 128
DILATED_PATTERNS = ((128, 1), (512, 4), (2048, 16))
REL_BUCKETS = 32
REL_MAX_EXACT = 16
REL_MAX_DIST = 2048
MEM_HEADS = 4
MOE_GROUPS = 4
EXPERTS_PER_GROUP = 8
N_EXPERTS = MOE_GROUPS * EXPERTS_PER_GROUP
RMS_EPS = 1e-6
NEG_INF = -1e30

LANES = 128
SUBLANES = 8
VMEM_LIMIT = 48 * 1024 * 1024

MLSTM_PAD = MLSTM_HEADS * LANES
MLSTM_CHUNK = 128
S5_BLOCK = 256
MOE_ROWS = 256
TOKEN_BLOCK = 512


def _cparams(*sem):
    return pltpu.CompilerParams(dimension_semantics=sem, vmem_limit_bytes=VMEM_LIMIT)


def _rms(x, gain):
    return x * lax.rsqrt(jnp.mean(x * x, axis=-1, keepdims=True) + RMS_EPS) * gain


def _dot(a, b):
    return jnp.dot(a, b, preferred_element_type=f32)


def _dot_nt(a, b):
    return lax.dot_general(a, b, (((1,), (1,)), ((), ())), preferred_element_type=f32)


def _dot_tn(a, b):
    return lax.dot_general(a, b, (((0,), (0,)), ((), ())), preferred_element_type=f32)


def _dot_exact(a, b):
    return jnp.dot(a, b, preferred_element_type=f32, precision=lax.Precision.HIGHEST)


def _full(shape):
    n = len(shape)
    return pl.BlockSpec(shape, lambda *_: (0,) * n)


IN_SEGS = (MLSTM_PAD, MLSTM_PAD, LANES, S5_WIDTH, ATTN_WIDTH, ATTN_WIDTH, ATTN_WIDTH)


DILATIONS = tuple(dil for _, dil in DILATED_PATTERNS)
ATTN_TILES = ATTN_WIDTH // LANES


def _in_proj_kernel(x_ref, g_ref, w_ref, mu_ref, mo_ref, gt_ref, su_ref, *rest):
    qkv_refs, scr_ref = rest[:-1], rest[-1]
    tm = x_ref.shape[0]
    xn = _rms(x_ref[...], g_ref[...]).astype(bf16)
    off = 0
    for o_ref, width in zip((mu_ref, mo_ref, gt_ref, su_ref), IN_SEGS[:4]):
        o_ref[...] = _dot(xn, w_ref[:, off:off + width]).astype(o_ref.dtype)
        off += width
    nd = len(DILATIONS)
    for a in range(3):
        val = _dot(xn, w_ref[:, off:off + ATTN_WIDTH])
        off += ATTN_WIDTH
        for c in range(ATTN_TILES):
            scr_ref[a, c] = val[:, c * LANES:(c + 1) * LANES]
        for o_ref, dil in zip(qkv_refs[a * nd:(a + 1) * nd], DILATIONS):
            if dil == 1:
                o_ref[...] = val.astype(o_ref.dtype)
                continue
            for r in range(dil):
                for c in range(ATTN_TILES):
                    col = r * ATTN_WIDTH + c * LANES
                    o_ref[:, col:col + LANES] = scr_ref[a, c, pl.ds(r, tm // dil, stride=dil), :].astype(o_ref.dtype)


def _in_proj(x2, gain, w_cat):
    t, d = x2.shape
    tm = min(TOKEN_BLOCK, t)
    widths = IN_SEGS[:4]
    shapes = [(t, w, tm, f32) for w in widths]
    shapes += [(t // dil, dil * ATTN_WIDTH, tm // dil, bf16) for _ in range(3) for dil in DILATIONS]
    return pl.pallas_call(
        _in_proj_kernel,
        grid=(t // tm,),
        in_specs=[pl.BlockSpec((tm, d), lambda i: (i, 0)), _full((1, d)), _full(w_cat.shape)],
        out_specs=[pl.BlockSpec((rows, w), lambda i: (i, 0)) for _, w, rows, _ in shapes],
        out_shape=[jax.ShapeDtypeStruct((n, w), dt) for n, w, _, dt in shapes],
        scratch_shapes=[pltpu.VMEM((3, ATTN_TILES, tm, LANES), f32)],
        compiler_params=_cparams("parallel"),
        name="in_proj",
    )(x2, gain, w_cat)


def _pad_heads(w, axis):
    shape = w.shape
    h = shape[axis] // HEAD_DIM
    w = w.reshape(shape[:axis] + (h, HEAD_DIM) + shape[axis + 1:])
    pad = [(0, 0)] * w.ndim
    pad[axis + 1] = (0, LANES - HEAD_DIM)
    w = jnp.pad(w, pad)
    return w.reshape(shape[:axis] + (h * LANES,) + shape[axis + 1:])


def _block_diag_heads(w):
    h = w.shape[0]
    wp = jnp.pad(w, ((0, 0), (0, LANES - HEAD_DIM), (0, LANES - HEAD_DIM)))
    eye = jnp.eye(h, dtype=w.dtype)
    return (wp[:, :, None, :] * eye[:, None, :, None]).reshape(h * LANES, h * LANES)


def _log_sigmoid(x):
    return jnp.minimum(x, 0.0) - jnp.log1p(jnp.exp(-jnp.abs(x)))


def _mlstm_kernel(mu_ref, mo_ref, gc_ref, gr_ref, cw_ref, cb_ref, wq_ref, wk_ref, wv_ref,
                  gbr_ref, gbc_ref, ng_ref, sk_ref, y_ref, uext_ref, c_ref, m_ref):
    L = MLSTM_CHUNK
    H = MLSTM_HEADS

    @pl.when(pl.program_id(1) == 0)
    def _():
        uext_ref[0:SUBLANES, :] = jnp.zeros((SUBLANES, MLSTM_PAD), f32)
        c_ref[...] = jnp.zeros_like(c_ref)
        m_ref[...] = jnp.zeros_like(m_ref)

    u = mu_ref[...]
    uext_ref[SUBLANES:SUBLANES + L, :] = u
    acc = jnp.broadcast_to(cb_ref[...], (L, MLSTM_PAD))
    for k in range(CONV_WIDTH):
        start = SUBLANES - (CONV_WIDTH - 1) + k
        acc = acc + cw_ref[k:k + 1, :] * uext_ref[start:start + L, :]
    c = acc * jax.nn.sigmoid(acc)
    uext_ref[0:SUBLANES, :] = u[L - SUBLANES:L, :]

    cb = c.astype(bf16)
    q = _dot(cb, wq_ref[...]).astype(bf16)
    kf = _dot(cb, wk_ref[...])
    vf = _dot(u.astype(bf16), wv_ref[...])

    gcol = gc_ref[...] + gbr_ref[...]
    grow = gr_ref[...] + gbc_ref[...]
    lf_col = _log_sigmoid(gcol)
    lf_row = _log_sigmoid(grow)
    ri = lax.broadcasted_iota(i32, (L, L), 0)
    ci = lax.broadcasted_iota(i32, (L, L), 1)
    causal = ci <= ri
    b_col = _dot_exact(causal.astype(f32), lf_col)
    b_row = _dot_exact(lf_row, (ri <= ci).astype(f32))
    lane = lax.broadcasted_iota(i32, (L, LANES), 1)
    o_gate = jax.nn.sigmoid(mo_ref[...])

    for h in range(H):
        sl = slice(h * LANES, (h + 1) * LANES)
        bc = b_col[:, H + h:H + h + 1]
        br = b_row[H + h:H + h + 1, :]
        lic = gcol[:, h:h + 1]
        lir = grow[h:h + 1, :]
        g = bc[L - 1:L, :]
        m_prev = m_ref[h][0:1, 0:1]
        c_prev = c_ref[h]

        d = jnp.where(causal, bc - br + lir, -jnp.inf)
        m_inter = bc + m_prev
        m_t = jnp.maximum(m_inter, jnp.max(d, axis=-1, keepdims=True))
        qh = q[:, sl]
        kh = kf[:, sl]
        v_aug = jnp.where(lane == HEAD_DIM, 1.0, vf[:, sl]).astype(bf16)
        w_intra = jnp.exp(d - m_t) * _dot_nt(qh, kh.astype(bf16))
        inter = jnp.exp(m_inter - m_t)
        nd = inter * _dot(qh, c_prev.astype(bf16)) + _dot(w_intra.astype(bf16), v_aug)
        den = nd[:, HEAD_DIM:HEAD_DIM + 1]
        hh = nd / jnp.maximum(jnp.abs(den), jnp.exp(-m_t))
        hh = jnp.where(lane < HEAD_DIM, hh, 0.0)
        hn = hh * lax.rsqrt(jnp.sum(hh * hh, axis=-1, keepdims=True) * (1.0 / HEAD_DIM) + RMS_EPS)
        y = o_gate[:, sl] * (hn * ng_ref[:, sl] + sk_ref[:, sl] * c[:, sl])
        y_ref[:, sl] = y.astype(y_ref.dtype)

        a = g - bc + lic
        m_new = jnp.maximum(g + m_prev, jnp.max(a, axis=0, keepdims=True))
        decay = jnp.exp(g + m_prev - m_new)
        kw = (kh * jnp.exp(a - m_new)).astype(bf16)
        c_ref[h] = decay * c_prev + _dot_tn(kw, v_aug)
        m_ref[h] = jnp.broadcast_to(m_new, (SUBLANES, LANES))


def _mlstm(mu, mo, gates, p, batch):
    t = mu.shape[0]
    s = t // batch
    L = MLSTM_CHUNK
    nc = s // L
    g_row = jnp.transpose(gates.reshape(batch, s, LANES)[:, :, :SUBLANES], (0, 2, 1))
    tok = lambda b, c: (b * nc + c, 0)
    w_spec = _full((MLSTM_PAD, MLSTM_PAD))
    v_spec = _full((1, MLSTM_PAD))
    return pl.pallas_call(
        _mlstm_kernel,
        grid=(batch, nc),
        in_specs=[pl.BlockSpec((L, MLSTM_PAD), tok), pl.BlockSpec((L, MLSTM_PAD), tok),
                  pl.BlockSpec((L, LANES), tok),
                  pl.BlockSpec((None, SUBLANES, L), lambda b, c: (b, 0, c)),
                  _full((CONV_WIDTH, MLSTM_PAD)), v_spec, w_spec, w_spec, w_spec,
                  _full((1, LANES)), _full((SUBLANES, 1)), v_spec, v_spec],
        out_specs=pl.BlockSpec((L, MLSTM_PAD), tok),
        out_shape=jax.ShapeDtypeStruct((t, MLSTM_PAD), bf16),
        scratch_shapes=[pltpu.VMEM((L + SUBLANES, MLSTM_PAD), f32),
                        pltpu.VMEM((MLSTM_HEADS, LANES, LANES), f32),
                        pltpu.VMEM((MLSTM_HEADS, SUBLANES, LANES), f32)],
        compiler_params=_cparams("arbitrary", "arbitrary"),
        name="mlstm",
    )(mu, mo, gates, g_row, p["conv_w"], p["conv_b"], p["wq"], p["wk"], p["wv"],
      p["gb_row"], p["gb_col"], p["norm"], p["skip"])


def _s5_param_kernel(are_ref, aim_ref, ldt_ref, bre_ref, bim_ref, pwr_ref, pwi_ref, bbr_ref, bbi_ref):
    a_re = are_ref[...]
    a_im = aim_ref[...]
    dt = jnp.exp(ldt_ref[...])
    mag = jnp.exp(a_re * dt)
    ab_re = mag * jnp.cos(a_im * dt)
    ab_im = mag * jnp.sin(a_im * dt)
    den = a_re * a_re + a_im * a_im
    xr, yi = ab_re - 1.0, ab_im
    coef_re = (xr * a_re + yi * a_im) / den
    coef_im = (yi * a_re - xr * a_im) / den
    b_re = bre_ref[...]
    b_im = bim_ref[...]
    bbr_ref[...] = coef_re[:, None, :] * b_re - coef_im[:, None, :] * b_im
    bbi_ref[...] = coef_re[:, None, :] * b_im + coef_im[:, None, :] * b_re
    pr, pi = ab_re, ab_im
    pwr_ref[0] = pr
    pwi_ref[0] = pi
    for k in range(1, SUBLANES):
        pr, pi = pr * ab_re - pi * ab_im, pr * ab_im + pi * ab_re
        pwr_ref[k] = pr
        pwi_ref[k] = pi


def _s5_params(a_re, a_im, log_dt, b_re, b_im):
    G, P, Hc = S5_GROUPS, S5_STATE, S5_GROUP_CH
    bt_re = jnp.transpose(b_re, (0, 2, 1))
    bt_im = jnp.transpose(b_im, (0, 2, 1))
    return pl.pallas_call(
        _s5_param_kernel,
        out_shape=[jax.ShapeDtypeStruct((SUBLANES, G, P), f32), jax.ShapeDtypeStruct((SUBLANES, G, P), f32),
                   jax.ShapeDtypeStruct((G, Hc, P), f32), jax.ShapeDtypeStruct((G, Hc, P), f32)],
        name="s5_params",
    )(a_re, a_im, log_dt.reshape(G, 1), bt_re, bt_im)


def _gelu_tanh(x):
    return 0.5 * x * (1.0 + jnp.tanh(math.sqrt(2.0 / math.pi) * (x + 0.044715 * (x * x * x))))


def _s5_kernel(u_ref, wb_ref, pwr_ref, pwi_ref, wc_ref, dsk_ref, wg_ref, bg_ref, ng_ref, y_ref,
               sr_ref, si_ref, cr_ref, ci_ref):
    TT = u_ref.shape[0]
    NL = S5_LANES
    ng8 = TT // SUBLANES

    @pl.when(pl.program_id(1) == 0)
    def _():
        cr_ref[...] = jnp.zeros_like(cr_ref)
        ci_ref[...] = jnp.zeros_like(ci_ref)

    u = u_ref[...]
    bu = _dot(u.astype(bf16), wb_ref[...])
    xr = bu[:, :NL].reshape(ng8, SUBLANES, NL)
    xi = bu[:, NL:].reshape(ng8, SUBLANES, NL)
    row = lax.broadcasted_iota(i32, (1, SUBLANES, NL), 1)
    for sh in (1, 2, 4):
        ar = pwr_ref[sh - 1:sh, :][None]
        ai = pwi_ref[sh - 1:sh, :][None]
        tr = pltpu.roll(xr, sh, 1)
        ti = pltpu.roll(xi, sh, 1)
        keep = row >= sh
        xr, xi = (xr + jnp.where(keep, ar * tr - ai * ti, 0.0),
                  xi + jnp.where(keep, ar * ti + ai * tr, 0.0))
    sr_ref[...] = xr.reshape(TT, NL)
    si_ref[...] = xi.reshape(TT, NL)

    pr = pwr_ref[...]
    pi = pwi_ref[...]

    def group(gidx, carry):
        c_re, c_im = carry
        r0 = pl.multiple_of(gidx * SUBLANES, SUBLANES)
        t_re = sr_ref[pl.ds(r0, SUBLANES), :] + pr * c_re - pi * c_im
        t_im = si_ref[pl.ds(r0, SUBLANES), :] + pr * c_im + pi * c_re
        sr_ref[pl.ds(r0, SUBLANES), :] = t_re
        si_ref[pl.ds(r0, SUBLANES), :] = t_im
        return t_re[SUBLANES - 1:SUBLANES, :], t_im[SUBLANES - 1:SUBLANES, :]

    c_re, c_im = lax.fori_loop(0, ng8, group, (cr_ref[...], ci_ref[...]))
    cr_ref[...] = c_re
    ci_ref[...] = c_im

    y = (_dot(sr_ref[...].astype(bf16), wc_ref[0:NL, :]) + _dot(si_ref[...].astype(bf16), wc_ref[NL:2 * NL, :])
         + dsk_ref[...] * u)
    z = _gelu_tanh(y)
    out = z * jax.nn.sigmoid(_dot(z.astype(bf16), wg_ref[...]) + bg_ref[...])
    y_ref[...] = _rms(out, ng_ref[...]).astype(y_ref.dtype)


def _s5(su, p, batch):
    t = su.shape[0]
    s = t // batch
    tt = min(S5_BLOCK, s)
    nb = s // tt
    W, NL = S5_WIDTH, S5_LANES
    tok = lambda b, c: (b * nb + c, 0)
    return pl.pallas_call(
        _s5_kernel,
        grid=(batch, nb),
        in_specs=[pl.BlockSpec((tt, W), tok), _full((W, 2 * NL)), _full((SUBLANES, NL)), _full((SUBLANES, NL)),
                  _full((2 * NL, W)), _full((1, W)), _full((W, W)), _full((1, W)), _full((1, W))],
        out_specs=pl.BlockSpec((tt, W), tok),
        out_shape=jax.ShapeDtypeStruct((t, W), bf16),
        scratch_shapes=[pltpu.VMEM((tt, NL), f32), pltpu.VMEM((tt, NL), f32),
                        pltpu.VMEM((1, NL), f32), pltpu.VMEM((1, NL), f32)],
        compiler_params=_cparams("arbitrary", "arbitrary"),
        name="s5",
    )(su, p["wb"], p["pw_re"], p["pw_im"], p["wc"], p["d"], p["w_glu"], p["b_glu"], p["norm"])


def _t5_bucket(dist):
    exact = dist < REL_MAX_EXACT
    large = REL_MAX_EXACT + (np.log(np.maximum(dist, 1) / REL_MAX_EXACT)
                             / np.log(REL_MAX_DIST / REL_MAX_EXACT)
                             * (REL_BUCKETS - REL_MAX_EXACT)).astype(np.int32)
    large = np.minimum(large, REL_BUCKETS - 1)
    return np.where(exact, dist, large).astype(np.int32)


def _band_offsets():
    i = np.arange(ATTN_BLOCK)[:, None]
    j = np.arange(2 * ATTN_BLOCK)[None, :]
    return ATTN_BLOCK + i - j


def _attn_bias(rel_table, dil):
    bucket = _t5_bucket(np.clip(_band_offsets(), 0, None) * dil)
    return jnp.transpose(rel_table.astype(f32)[bucket], (2, 0, 1))


def _attn_kernel(q_ref, kc_ref, kp_ref, vc_ref, vp_ref, bias_ref, o_ref, lse_ref, kcat_ref, vcat_ref,
                 s_ref, p_ref, *, n_back):
    RB = q_ref.shape[0]
    BLK = ATTN_BLOCK
    n = pl.program_id(2)
    kcat_ref[0:BLK, :] = kp_ref[...]
    kcat_ref[BLK:BLK + RB, :] = kc_ref[...]
    vcat_ref[0:BLK, :] = vp_ref[...]
    vcat_ref[BLK:BLK + RB, :] = vc_ref[...]

    ri = lax.broadcasted_iota(i32, (BLK, 2 * BLK), 0)
    ci = lax.broadcasted_iota(i32, (BLK, 2 * BLK), 1)
    off = BLK + ri - ci
    band = (off >= 0) & (off <= n_back)
    lane_q = lax.broadcasted_iota(i32, (BLK, LANES), 1)
    lane_kv = lax.broadcasted_iota(i32, (2 * BLK, LANES), 1)

    upper_q = lane_q >= HEAD_DIM
    upper_kv = lane_kv >= HEAD_DIM
    pairs = ATTN_HEADS // 2

    def sub_block(j, _):
        r0 = pl.multiple_of(j * BLK, BLK)
        has_prev = (n > 0) | (j > 0)
        valid = band & (has_prev | (ci >= BLK))
        for hp in range(pairs):
            cs = slice(hp * LANES, (hp + 1) * LANES)
            qp = q_ref[pl.ds(r0, BLK), cs]
            kp = kcat_ref[pl.ds(r0, 2 * BLK), cs]
            for e in range(2):
                qm = jnp.where(upper_q == (e == 1), qp, jnp.zeros_like(qp))
                s = _dot_nt(qm, kp) + bias_ref[2 * hp + e]
                s_ref[2 * hp + e] = jnp.where(valid, s, NEG_INF)
        ms = [jnp.max(s_ref[h], axis=-1, keepdims=True) for h in range(ATTN_HEADS)]
        for h in range(ATTN_HEADS):
            p_ref[h] = jnp.exp(s_ref[h] - ms[h]).astype(bf16)
        for hp in range(pairs):
            cs = slice(hp * LANES, (hp + 1) * LANES)
            vp = vcat_ref[pl.ds(r0, 2 * BLK), cs]
            o_pair = None
            lse_pair = None
            for e in range(2):
                mine = upper_kv == (e == 1)
                ones_lane = (1 - e) * HEAD_DIM
                vm = jnp.where(mine, vp, jnp.where(lane_kv == ones_lane, 1.0, 0.0).astype(bf16))
                o_e = _dot(p_ref[2 * hp + e], vm)
                l = o_e[:, ones_lane:ones_lane + 1]
                o_e = o_e / l
                lse_e = jnp.broadcast_to(ms[2 * hp + e] + jnp.log(l), (BLK, LANES))
                o_pair = o_e if o_pair is None else jnp.where(upper_q, o_e, o_pair)
                lse_pair = lse_e if lse_pair is None else jnp.where(upper_q, lse_e, lse_pair)
            o_ref[pl.ds(r0, BLK), cs] = o_pair
            lse_ref[pl.ds(r0, BLK), cs] = lse_pair
        return 0

    lax.fori_loop(0, RB // BLK, sub_block, 0)


def _dilated_attn(q, k, v, bias, window, dil, batch):
    w = ATTN_WIDTH
    n = q.shape[0] // batch
    rb = min(TOKEN_BLOCK, n)
    nblk = n // rb
    per = rb // ATTN_BLOCK
    view = lambda a: a.reshape(batch, n, dil * w)
    cur = pl.BlockSpec((None, rb, w), lambda b, r, i: (b, i, r))
    prev = pl.BlockSpec((None, ATTN_BLOCK, w), lambda b, r, i: (b, jnp.maximum(i * per - 1, 0), r))
    out = pl.BlockSpec((None, rb, w), lambda b, r, i: (b, i, r))
    o, lse = pl.pallas_call(
        functools.partial(_attn_kernel, n_back=window // dil),
        grid=(batch, dil, nblk),
        in_specs=[cur, cur, prev, cur, prev, _full(bias.shape)],
        out_specs=[out, out],
        out_shape=[jax.ShapeDtypeStruct((batch, n, dil * w), f32)] * 2,
        scratch_shapes=[pltpu.VMEM((rb + ATTN_BLOCK, w), bf16), pltpu.VMEM((rb + ATTN_BLOCK, w), bf16),
                        pltpu.VMEM((ATTN_HEADS, ATTN_BLOCK, 2 * ATTN_BLOCK), f32),
                        pltpu.VMEM((ATTN_HEADS, ATTN_BLOCK, 2 * ATTN_BLOCK), bf16)],
        compiler_params=_cparams("parallel", "parallel", "arbitrary"),
        name=f"dilated_attn_{dil}",
    )(view(q), view(k), view(k), view(v), view(v), bias)
    return o.reshape(batch * n, dil * w), lse.reshape(batch * n, dil * w)


def _token_order(ref, scr_ref, slot, dil, tm):
    if dil == 1:
        return ref[...]
    for r in range(dil):
        for c in range(ATTN_TILES):
            col = r * ATTN_WIDTH + c * LANES
            scr_ref[slot, c, pl.ds(r, tm // dil, stride=dil), :] = ref[:, col:col + LANES]
    return jnp.concatenate([scr_ref[slot, c] for c in range(ATTN_TILES)], axis=1)


def _out_proj_kernel(x_ref, ym_ref, ys_ref, o1_ref, o2_ref, o3_ref, l1_ref, l2_ref, l3_ref, ag_ref,
                     wm_ref, ws_ref, wa_ref, out_ref, scr_ref):
    tm = x_ref.shape[0]
    o_refs, l_refs = (o1_ref, o2_ref, o3_ref), (l1_ref, l2_ref, l3_ref)
    o = [_token_order(r, scr_ref, 2 * i, dil, tm) for i, (r, dil) in enumerate(zip(o_refs, DILATIONS))]
    lse = [_token_order(r, scr_ref, 2 * i + 1, dil, tm) for i, (r, dil) in enumerate(zip(l_refs, DILATIONS))]
    mx = jnp.maximum(jnp.maximum(lse[0], lse[1]), lse[2])
    e = [jnp.exp(l - mx) for l in lse]
    tot = e[0] + e[1] + e[2]
    ya = (e[0] / tot) * o[0] + (e[1] / tot) * o[1] + (e[2] / tot) * o[2]
    ya = _rms(ya, ag_ref[...]).astype(bf16)
    out_ref[...] = (x_ref[...] + _dot(ym_ref[...], wm_ref[...]) + _dot(ys_ref[...], ws_ref[...])
                    + _dot(ya, wa_ref[...]))


def _out_proj(x2, ym, ys, attn, a_gain, wm, ws, wa):
    t, d = x2.shape
    tm = min(TOKEN_BLOCK, t)
    row = lambda w: pl.BlockSpec((tm, w), lambda i: (i, 0))
    dilated = [pl.BlockSpec((tm // dil, dil * ATTN_WIDTH), lambda i: (i, 0)) for dil in DILATIONS]
    (o1, l1), (o2, l2), (o3, l3) = attn
    return pl.pallas_call(
        _out_proj_kernel,
        grid=(t // tm,),
        in_specs=[row(d), row(MLSTM_PAD), row(S5_WIDTH)] + dilated + dilated
                 + [_full((1, ATTN_WIDTH)), _full(wm.shape), _full(ws.shape), _full(wa.shape)],
        out_specs=row(d),
        out_shape=jax.ShapeDtypeStruct((t, d), f32),
        scratch_shapes=[pltpu.VMEM((2 * len(DILATIONS), ATTN_TILES, tm, LANES), f32)],
        compiler_params=_cparams("parallel"),
        name="out_proj",
    )(x2, ym, ys, o1, o2, o3, l1, l2, l3, a_gain, wm, ws, wa)


def _mem_kv_kernel(mem_ref, g_ref, w_ref, kv_ref):
    kv_ref[...] = _dot(_rms(mem_ref[...], g_ref[...]).astype(bf16), w_ref[...]).astype(kv_ref.dtype)


def _mem_kv(mem2, gain, w_kv):
    r, d = mem2.shape
    return pl.pallas_call(
        _mem_kv_kernel,
        out_shape=jax.ShapeDtypeStruct((r, 2 * d), bf16),
        compiler_params=pltpu.CompilerParams(vmem_limit_bytes=VMEM_LIMIT),
        name="mem_kv",
    )(mem2, gain, w_kv)


def _xattn_kernel(x_ref, g_ref, wq_ref, kv_ref, wo_ref, out_ref):
    x = x_ref[...]
    d = x.shape[1]
    hd = d // MEM_HEADS
    q = _dot(_rms(x, g_ref[...]).astype(bf16), wq_ref[...]).astype(bf16)
    heads = []
    for h in range(MEM_HEADS):
        kh = kv_ref[:, h * hd:(h + 1) * hd]
        vh = kv_ref[:, d + h * hd:d + (h + 1) * hd]
        s = _dot_nt(q[:, h * hd:(h + 1) * hd], kh) * (hd ** -0.5)
        m = jnp.max(s, axis=-1, keepdims=True)
        p = jnp.exp(s - m)
        p = p / jnp.sum(p, axis=-1, keepdims=True)
        heads.append(_dot(p.astype(bf16), vh).astype(bf16))
    o = jnp.concatenate(heads, axis=-1)
    out_ref[...] = x + _dot(o, wo_ref[...])


def _xattn(x2, gain, w_q, kv, w_o, batch):
    t, d = x2.shape
    s = t // batch
    m = kv.shape[0] // batch
    tm = min(TOKEN_BLOCK, s)
    nb = s // tm
    tok = lambda b, i: (b * nb + i, 0)
    return pl.pallas_call(
        _xattn_kernel,
        grid=(batch, nb),
        in_specs=[pl.BlockSpec((tm, d), tok), _full((1, d)), _full((d, d)),
                  pl.BlockSpec((m, 2 * d), lambda b, i: (b, 0)), _full((d, d))],
        out_specs=pl.BlockSpec((tm, d), tok),
        out_shape=jax.ShapeDtypeStruct((t, d), f32),
        compiler_params=_cparams("parallel", "parallel"),
        name="xattn",
    )(x2, gain, w_q, kv, w_o)


MOE_UNIT = SUBLANES
MOE_BLOCK_UNITS = MOE_ROWS // MOE_UNIT
GATE_PIECES = 3


def _moe_slots(tm):
    worst = 2 * tm + N_EXPERTS * (MOE_UNIT - 1)
    return -(-worst // LANES) * LANES


def _first_argmax(x, lane):
    m = jnp.max(x, axis=-1, keepdims=True)
    idx = jnp.min(jnp.where(x == m, lane, LANES), axis=-1, keepdims=True)
    return m, idx


def _router_kernel(x_ref, g_ref, wr_ref, br_ref, xn_ref, sel_ref, gate_ref, cnt_ref):
    TM = x_ref.shape[0]
    G, EPG, E = MOE_GROUPS, EXPERTS_PER_GROUP, N_EXPERTS
    xn = _rms(x_ref[...], g_ref[...])
    xn_ref[...] = xn.astype(xn_ref.dtype)
    logits = _dot_exact(xn, wr_ref[...]) + br_ref[...]
    lane = lax.broadcasted_iota(i32, (TM, LANES), 1)
    is_group = (lane >= E) & (lane < E + G)
    gl = jnp.where(is_group, logits, -jnp.inf)
    gmax, gidx = _first_argmax(gl, lane)
    g_p = 1.0 / jnp.sum(jnp.exp(gl - gmax), axis=-1, keepdims=True)
    g_i = gidx - E
    in_group = (lane < E) & ((lane >> int(math.log2(EPG))) == g_i)
    el = jnp.where(in_group, logits, -jnp.inf)
    m1, i1 = _first_argmax(el, lane)
    m2, i2 = _first_argmax(jnp.where(lane == i1, -jnp.inf, el), lane)
    p2 = jnp.exp(m2 - m1)
    gate1 = g_p / (1.0 + p2)
    gate2 = g_p * p2 / (1.0 + p2)

    onehot = ((lane == i1) | (lane == i2)).astype(bf16)
    r = lax.broadcasted_iota(i32, (TM, TM), 0)
    c = lax.broadcasted_iota(i32, (TM, TM), 1)
    before = _dot((c < r).astype(bf16), onehot)
    cnt = jnp.sum(onehot.astype(f32), axis=0, keepdims=True)
    padded = jnp.floor((cnt + (MOE_UNIT - 1)) * (1.0 / MOE_UNIT)) * MOE_UNIT
    er = lax.broadcasted_iota(i32, (LANES, LANES), 0)
    ec = lax.broadcasted_iota(i32, (LANES, LANES), 1)
    padded8 = jnp.broadcast_to(padded, (SUBLANES, LANES))
    seg_start = _dot_exact(padded8, (er < ec).astype(f32))[0:1, :]
    slot = seg_start + before
    pos1 = jnp.sum(jnp.where(lane == i1, slot, 0.0), axis=-1, keepdims=True)
    pos2 = jnp.sum(jnp.where(lane == i2, slot, 0.0), axis=-1, keepdims=True)
    sel_ref[...] = jnp.where(lane == 0, pos1, jnp.where(lane == 1, pos2, 0.0))
    gate_ref[...] = jnp.where(lane == 0, gate1, jnp.where(lane == 1, gate2, 0.0))
    cnt_ref[...] = padded8.astype(i32)


def _router(x2, gain, w_r, b_r):
    t, d = x2.shape
    tm = min(TOKEN_BLOCK, t)
    nblk = t // tm
    row = lambda w: pl.BlockSpec((tm, w), lambda i: (i, 0))
    return pl.pallas_call(
        _router_kernel,
        grid=(nblk,),
        in_specs=[row(d), _full((1, d)), _full((d, LANES)), _full((1, LANES))],
        out_specs=[row(d), row(LANES), row(LANES), pl.BlockSpec((None, SUBLANES, LANES), lambda i: (i, 0, 0))],
        out_shape=[jax.ShapeDtypeStruct((t, d), bf16), jax.ShapeDtypeStruct((t, LANES), f32),
                   jax.ShapeDtypeStruct((t, LANES), f32), jax.ShapeDtypeStruct((nblk, SUBLANES, LANES), i32)],
        compiler_params=_cparams("parallel"),
        name="moe_router",
    )(x2, gain, w_r, b_r)


def _unit_copy(src_ref, src_unit, dst_ref, dst_unit, sem):
    src = src_ref.at[pl.ds(pl.multiple_of(src_unit * MOE_UNIT, MOE_UNIT), MOE_UNIT), :]
    dst = dst_ref.at[pl.ds(pl.multiple_of(dst_unit * MOE_UNIT, MOE_UNIT), MOE_UNIT), :]
    return pltpu.make_async_copy(src, dst, sem)


def _gate_pieces(g, lane):
    hi = g.astype(bf16).astype(f32)
    r1 = g - hi
    mid = r1.astype(bf16).astype(f32)
    lo = r1 - mid
    return jnp.where(lane == 0, hi, jnp.where(lane == 1, mid, jnp.where(lane == 2, lo, 0.0))).astype(bf16)


def _dispatch_kernel(unit_ref, tail_ref, xn_ref, sel_ref, gate_ref, rows_ref, sorted_ref, zero_ref, sem):
    TM, D = xn_ref.shape
    NS = sorted_ref.shape[0]
    NU = NS // MOE_UNIT
    tb = pl.program_id(0)

    @pl.when(tb == 0)
    def _():
        zero_ref[...] = jnp.zeros_like(zero_ref)

        def zero_copy(e):
            start = pl.multiple_of(tail_ref[e] * MOE_UNIT - MOE_ROWS, MOE_UNIT)
            return pltpu.make_async_copy(zero_ref, rows_ref.at[pl.ds(start, MOE_ROWS), :], sem)

        def issue(e, _):
            @pl.when(tail_ref[N_EXPERTS + e] > 0)
            def _():
                zero_copy(e).start()
            return 0

        def drain(e, _):
            @pl.when(tail_ref[N_EXPERTS + e] > 0)
            def _():
                zero_copy(e).wait()
            return 0

        lax.fori_loop(0, N_EXPERTS, issue, 0)
        lax.fori_loop(0, N_EXPERTS, drain, 0)

        def block_copy(i):
            start = pl.multiple_of(i * MOE_ROWS, MOE_ROWS)
            return pltpu.make_async_copy(zero_ref, rows_ref.at[pl.ds(start, MOE_ROWS), :], sem)

        def issue_block(i, _):
            block_copy(i).start()
            return 0

        def drain_block(i, _):
            block_copy(i).wait()
            return 0

        n_used = tail_ref[2 * N_EXPERTS]
        lax.fori_loop(n_used, rows_ref.shape[0] // MOE_ROWS, issue_block, 0)
        lax.fori_loop(n_used, rows_ref.shape[0] // MOE_ROWS, drain_block, 0)

    sel_t =jnp.transpose(sel_ref[...])
    slot = lax.broadcasted_iota(i32, (NS, TM), 0).astype(f32)
    p1 = jnp.where(slot == sel_t[0:1, :], 1.0, 0.0).astype(bf16)
    p2 = jnp.where(slot == sel_t[1:2, :], 1.0, 0.0).astype(bf16)
    lane = lax.broadcasted_iota(i32, (TM, LANES), 1)
    gates = gate_ref[...]
    sorted_ref[:, 0:D] = _dot(p1 + p2, xn_ref[...])
    sorted_ref[:, D:D + LANES] = (_dot(p1, _gate_pieces(gates[:, 0:1], lane))
                                  + _dot(p2, _gate_pieces(gates[:, 1:2], lane)))
    base = tb * NU

    def issue(j, _):
        u = unit_ref[base + j]

        @pl.when(u >= 0)
        def _():
            _unit_copy(sorted_ref, j, rows_ref, u, sem).start()
        return 0

    def drain(j, _):
        u = unit_ref[base + j]

        @pl.when(u >= 0)
        def _():
            _unit_copy(sorted_ref, j, rows_ref, u, sem).wait()
        return 0

    lax.fori_loop(0, NU, issue, 0)
    lax.fori_loop(0, NU, drain, 0)


def _dispatch(unit_map, tail, xn, sel, gates, n_rows):
    t, d = xn.shape
    tm = min(TOKEN_BLOCK, t)
    ns = _moe_slots(tm)
    row = lambda w: pl.BlockSpec((tm, w), lambda i, um, tl: (i, 0))
    return pl.pallas_call(
        _dispatch_kernel,
        grid_spec=pltpu.PrefetchScalarGridSpec(
            num_scalar_prefetch=2,
            grid=(t // tm,),
            in_specs=[row(d), row(LANES), row(LANES)],
            out_specs=pl.BlockSpec(memory_space=pl.ANY),
            scratch_shapes=[pltpu.VMEM((ns, d + LANES), f32), pltpu.VMEM((MOE_ROWS, d + LANES), f32),
                            pltpu.SemaphoreType.DMA(())],
        ),
        out_shape=jax.ShapeDtypeStruct((n_rows, d + LANES), f32),
        compiler_params=_cparams("arbitrary"),
        name="moe_dispatch",
    )(unit_map, tail, xn, sel, gates)


def _expert_kernel(be_ref, nu_ref, x_ref, wgu_ref, wd_ref, y_ref):
    i = pl.program_id(0)
    de, d = wd_ref.shape

    @pl.when(i < nu_ref[0])
    def _():
        gate = x_ref[:, d:d + 1]
        for k in range(1, GATE_PIECES):
            gate = gate + x_ref[:, d + k:d + k + 1]
        ab = _dot(x_ref[:, 0:d].astype(bf16), wgu_ref[...])
        a, b = ab[:, :de], ab[:, de:]
        hid = (a * jax.nn.sigmoid(a) * b).astype(bf16)
        y_ref[...] = _dot(hid, wd_ref[...]) * gate

    @pl.when(i >= nu_ref[0])
    def _():
        y_ref[...] = jnp.zeros_like(y_ref)


def _experts(block_e, n_used, rows, w_gu, w_down):
    n_rows, width = rows.shape
    nblk = n_rows // MOE_ROWS
    de, d = w_down.shape[1:]
    used = lambda i, nu: jnp.minimum(i, nu[0] - 1)
    return pl.pallas_call(
        _expert_kernel,
        grid_spec=pltpu.PrefetchScalarGridSpec(
            num_scalar_prefetch=2,
            grid=(nblk,),
            in_specs=[pl.BlockSpec((MOE_ROWS, width), lambda i, be, nu: (used(i, nu), 0)),
                      pl.BlockSpec((None, d, 2 * de), lambda i, be, nu: (be[i], 0, 0)),
                      pl.BlockSpec((None, de, d), lambda i, be, nu: (be[i], 0, 0))],
            out_specs=pl.BlockSpec((MOE_ROWS, d), lambda i, be, nu: (i, 0)),
        ),
        out_shape=jax.ShapeDtypeStruct((n_rows, d), f32),
        compiler_params=_cparams("arbitrary"),
        name="moe_experts",
    )(block_e, n_used, rows, w_gu, w_down)


def _combine_kernel(unit_ref, x_ref, sel_ref, fg_ref, y_hbm, out_ref, ybuf, sem, *, final_norm):
    TM = x_ref.shape[0]
    NS = ybuf.shape[0]
    NU = NS // MOE_UNIT
    base = pl.program_id(0) * NU

    def issue(j, _):
        u = unit_ref[base + j]

        @pl.when(u >= 0)
        def _():
            _unit_copy(y_hbm, u, ybuf, j, sem).start()

        @pl.when(u < 0)
        def _():
            ybuf[pl.ds(pl.multiple_of(j * MOE_UNIT, MOE_UNIT), MOE_UNIT), :] = jnp.zeros((MOE_UNIT, ybuf.shape[1]), f32)
        return 0

    def drain(j, _):
        u = unit_ref[base + j]

        @pl.when(u >= 0)
        def _():
            _unit_copy(y_hbm, u, ybuf, j, sem).wait()
        return 0

    lax.fori_loop(0, NU, issue, 0)
    lax.fori_loop(0, NU, drain, 0)

    sel = sel_ref[...]
    slot = lax.broadcasted_iota(i32, (TM, NS), 1).astype(f32)
    pick = jnp.where((slot == sel[:, 0:1]) | (slot == sel[:, 1:2]), 1.0, 0.0).astype(bf16)
    y = ybuf[...]
    hi = y.astype(bf16)
    lo = (y - hi.astype(f32)).astype(bf16)
    out = x_ref[...] + _dot(pick, hi) + _dot(pick, lo)
    if final_norm:
        out = _rms(out, fg_ref[...])
    out_ref[...] = out


def _combine(unit_map, x2, sel, y_rows, final_gain, final_norm):
    t, d = x2.shape
    tm = min(TOKEN_BLOCK, t)
    ns = _moe_slots(tm)
    return pl.pallas_call(
        functools.partial(_combine_kernel, final_norm=final_norm),
        grid_spec=pltpu.PrefetchScalarGridSpec(
            num_scalar_prefetch=1,
            grid=(t // tm,),
            in_specs=[pl.BlockSpec((tm, d), lambda i, um: (i, 0)),
                      pl.BlockSpec((tm, LANES), lambda i, um: (i, 0)),
                      pl.BlockSpec((1, d), lambda i, um: (0, 0)),
                      pl.BlockSpec(memory_space=pl.ANY)],
            out_specs=pl.BlockSpec((tm, d), lambda i, um: (i, 0)),
            scratch_shapes=[pltpu.VMEM((ns, d), f32), pltpu.SemaphoreType.DMA(())],
        ),
        out_shape=jax.ShapeDtypeStruct((t, d), f32),
        compiler_params=_cparams("arbitrary"),
        name="moe_combine",
    )(unit_map, x2, sel, final_gain, y_rows)


def _moe(x2, gain, w_r, b_r, w_gu, w_down, final_gain, final_norm):
    t, d = x2.shape
    tm = min(TOKEN_BLOCK, t)
    nblk = t // tm
    nu = _moe_slots(tm) // MOE_UNIT
    xn, sel, gates, cnt = _router(x2, gain, w_r, b_r)
    units = cnt[:, 0, :N_EXPERTS] // MOE_UNIT
    seg_end = jnp.cumsum(units, axis=1)
    seg_start = seg_end - units
    run_off = jnp.cumsum(units, axis=0) - units
    total = jnp.sum(units, axis=0)
    region = (total + MOE_BLOCK_UNITS - 1) // MOE_BLOCK_UNITS * MOE_BLOCK_UNITS
    g_end = jnp.cumsum(region)
    g_start = g_end - region
    j = jnp.arange(nu, dtype=i32)
    owner = jnp.sum((seg_end[:, None, :] <= j[None, :, None]).astype(i32), axis=2)
    hit = owner[:, :, None] == jnp.arange(N_EXPERTS, dtype=i32)[None, None, :]
    shift = (g_start[None, :] + run_off - seg_start)[:, None, :]
    unit_map = jnp.where(owner < N_EXPERTS, j[None, :] + jnp.sum(jnp.where(hit, shift, 0), axis=2), -1)
    unit_map = unit_map.reshape(-1).astype(i32)
    tail = jnp.concatenate([g_end, region - total, g_end[-1:] // MOE_BLOCK_UNITS]).astype(i32)
    n_blocks = -(-(nblk * nu * MOE_UNIT + N_EXPERTS * (MOE_ROWS - MOE_UNIT)) // MOE_ROWS)
    block_start = jnp.arange(n_blocks, dtype=i32) * MOE_BLOCK_UNITS
    block_e = jnp.minimum(jnp.sum((g_end[None, :] <= block_start[:, None]).astype(i32), axis=1), N_EXPERTS - 1)
    n_used = (g_end[-1:] // MOE_BLOCK_UNITS).astype(i32)
    rows = _dispatch(unit_map, tail, xn, sel, gates, n_blocks * MOE_ROWS)
    y_rows = _experts(block_e, n_used, rows, w_gu, w_down)
    return _combine(unit_map, x2, sel, y_rows, final_gain, final_norm)


def _layer_params(l, a):
    H, W = MLSTM_HEADS, MLSTM_WIDTH
    w_in = a["w_in"][l]
    sizes = [W, W, H, H, S5_WIDTH, ATTN_WIDTH, ATTN_WIDTH, ATTN_WIDTH]
    offs = np.concatenate([[0], np.cumsum(sizes)])
    seg = lambda i: w_in[:, offs[i]:offs[i + 1]]
    gates_w = jnp.pad(jnp.concatenate([seg(2), seg(3)], axis=1), ((0, 0), (0, LANES - 2 * H)))
    w_cat = jnp.concatenate([_pad_heads(seg(0), 1), _pad_heads(seg(1), 1), gates_w, seg(4),
                             seg(5) * (HEAD_DIM ** -0.5), seg(6), seg(7)], axis=1).astype(bf16)
    row = lambda v: v.reshape(1, -1)
    gb = a["mlstm_gate_bias"][l]
    mlstm = dict(
        conv_w=_pad_heads(a["mlstm_conv_w"][l], 1), conv_b=row(_pad_heads(a["mlstm_conv_b"][l], 0)),
        wq=_block_diag_heads(a["mlstm_wq"][l] * (HEAD_DIM ** -0.5)).astype(bf16),
        wk=_block_diag_heads(a["mlstm_wk"][l]).astype(bf16),
        wv=_block_diag_heads(a["mlstm_wv"][l]).astype(bf16),
        gb_row=jnp.pad(gb, (0, LANES - 2 * H)).reshape(1, LANES), gb_col=gb.reshape(2 * H, 1),
        norm=row(_pad_heads(a["mlstm_norm"][l], 0)), skip=row(_pad_heads(a["mlstm_skip"][l], 0)))

    G, P, Hc = S5_GROUPS, S5_STATE, S5_GROUP_CH
    pw_re, pw_im, bb_re, bb_im = _s5_params(a["s5_a_re"][l], a["s5_a_im"][l], a["s5_log_dt"][l],
                                            a["s5_b_re"][l], a["s5_b_im"][l])
    eye = jnp.eye(G, dtype=f32)
    bd_in = lambda m: (m[:, :, None, :] * eye[:, None, :, None]).reshape(G * Hc, G * P)
    bd_out = lambda m: (jnp.transpose(m, (0, 2, 1))[:, :, None, :] * eye[:, None, :, None]).reshape(G * P, G * Hc)
    s5 = dict(
        wb=jnp.concatenate([bd_in(bb_re), bd_in(bb_im)], axis=1).astype(bf16),
        wc=jnp.concatenate([bd_out(a["s5_c_re"][l]), -bd_out(a["s5_c_im"][l])], axis=0).astype(bf16),
        pw_re=pw_re.reshape(SUBLANES, G * P), pw_im=pw_im.reshape(SUBLANES, G * P),
        d=a["s5_d"][l].reshape(1, S5_WIDTH), w_glu=a["s5_w_glu"][l].astype(bf16),
        b_glu=row(a["s5_b_glu"][l]), norm=row(a["s5_out_norm"][l]))

    w_out = a["w_out"][l]
    w_r = jnp.pad(jnp.concatenate([a["router_w_expert"][l], a["router_w_group"][l]], axis=1),
                  ((0, 0), (0, LANES - N_EXPERTS - MOE_GROUPS)))
    b_r = jnp.pad(jnp.concatenate([a["router_b_expert"][l], a["router_b_group"][l]]),
                  (0, LANES - N_EXPERTS - MOE_GROUPS)).reshape(1, LANES)
    return dict(
        norm_mix=row(a["norm_mix"][l]), w_cat=w_cat, mlstm=mlstm, s5=s5,
        attn_norm=row(a["attn_out_norm"][l]),
        wo_m=_pad_heads(w_out[:W], 0).astype(bf16), wo_s=w_out[W:W + S5_WIDTH].astype(bf16),
        wo_a=w_out[W + S5_WIDTH:].astype(bf16),
        norm_xattn=row(a["norm_xattn"][l]), norm_mem=row(a["norm_mem"][l]),
        xq=a["xattn_w_q"][l].astype(bf16), xkv=a["xattn_w_kv"][l].astype(bf16), xo=a["xattn_w_o"][l].astype(bf16),
        norm_ffn=row(a["norm_ffn"][l]), w_r=w_r, b_r=b_r,
        w_gu=a["expert_w_gu"][l].astype(bf16), w_down=a["expert_w_down"][l].astype(bf16))


def _forward(a):
    x = a["x"]
    batch, seq, d = x.shape
    depth = a["w_in"].shape[0]
    x2 = x.reshape(batch * seq, d)
    mem2 = a["mem"].reshape(-1, d)
    biases = [_attn_bias(a["rel_bias"], dil) for _, dil in DILATED_PATTERNS]
    final_gain = a["final_norm"].reshape(1, d)
    for l in range(depth):
        p = _layer_params(l, a)
        mu, mo, gates, su, *qkv = _in_proj(x2, p["norm_mix"], p["w_cat"])
        ym = _mlstm(mu, mo, gates, p["mlstm"], batch)
        ys = _s5(su, p["s5"], batch)
        nd = len(DILATIONS)
        attn = [_dilated_attn(qkv[i], qkv[nd + i], qkv[2 * nd + i], bias, window, dil, batch)
                for i, (bias, (window, dil)) in enumerate(zip(biases, DILATED_PATTERNS))]
        x2 = _out_proj(x2, ym, ys, attn, p["attn_norm"], p["wo_m"], p["wo_s"], p["wo_a"])
        kv = _mem_kv(mem2, p["norm_mem"], p["xkv"])
        x2 = _xattn(x2, p["norm_xattn"], p["xq"], kv, p["xo"], batch)
        x2 = _moe(x2, p["norm_ffn"], p["w_r"], p["b_r"], p["w_gu"], p["w_down"], final_gain,
                  final_norm=(l == depth - 1))
    return x2.reshape(batch, seq, d)


def kernel(x, mem, rel_bias, norm_mix, w_in, mlstm_conv_w, mlstm_conv_b, mlstm_wq, mlstm_wk, mlstm_wv,
           mlstm_gate_bias, mlstm_norm, mlstm_skip, s5_a_re, s5_a_im, s5_log_dt, s5_b_re, s5_b_im, s5_c_re,
           s5_c_im, s5_d, s5_w_glu, s5_b_glu, s5_out_norm, attn_out_norm, w_out, norm_xattn, norm_mem,
           xattn_w_q, xattn_w_kv, xattn_w_o, norm_ffn, router_w_group, router_b_group, router_w_expert,
           router_b_expert, expert_w_gu, expert_w_down, final_norm):
    return _forward(dict(locals()))
```

```python
import functools
import math

import numpy as np
import jax
import jax.numpy as jnp
from jax import lax
from jax.experimental import pallas as pl
from jax.experimental.pallas import tpu as pltpu

f32 = jnp.float32
bf16 = jnp.bfloat16
i32 = jnp.int32

HEAD_DIM = 64
MLSTM_HEADS = 4
MLSTM_WIDTH = MLSTM_HEADS * HEAD_DIM
CONV_WIDTH = 4
S5_GROUPS = 16
S5_GROUP_CH = 16
S5_STATE = 64
S5_WIDTH = S5_GROUPS * S5_GROUP_CH
S5_LANES = S5_GROUPS * S5_STATE
ATTN_HEADS = 8
ATTN_WIDTH = ATTN_HEADS * HEAD_DIM
ATTN_BLOCK = 128
DILATED_PATTERNS = ((128, 1), (512, 4), (2048, 16))
REL_BUCKETS = 32
REL_MAX_EXACT = 16
REL_MAX_DIST = 2048
MEM_HEADS = 4
MOE_GROUPS = 4
EXPERTS_PER_GROUP = 8
N_EXPERTS = MOE_GROUPS * EXPERTS_PER_GROUP
RMS_EPS = 1e-6
NEG_INF = -1e30

LANES = 128
SUBLANES = 8
VMEM_LIMIT = 48 * 1024 * 1024

MLSTM_PAD = MLSTM_HEADS * LANES
MLSTM_CHUNK = 128
MLSTM_STEP_CHUNKS = 2
S5_BLOCK = 256
MOE_ROWS = 512
TOKEN_BLOCK = 512


def _cparams(*sem):
    return pltpu.CompilerParams(dimension_semantics=sem, vmem_limit_bytes=VMEM_LIMIT)


def _rms(x, gain):
    return x * lax.rsqrt(jnp.mean(x * x, axis=-1, keepdims=True) + RMS_EPS) * gain


def _dot(a, b):
    return jnp.dot(a, b, preferred_element_type=f32)


def _dot_nt(a, b):
    return lax.dot_general(a, b, (((1,), (1,)), ((), ())), preferred_element_type=f32)


def _dot_tn(a, b):
    return lax.dot_general(a, b, (((0,), (0,)), ((), ())), preferred_element_type=f32)


def _dot_exact(a, b):
    return jnp.dot(a, b, preferred_element_type=f32, precision=lax.Precision.HIGHEST)


def _full(shape):
    n = len(shape)
    return pl.BlockSpec(shape, lambda *_: (0,) * n)


IN_SEGS = (MLSTM_PAD, MLSTM_PAD, LANES, S5_WIDTH, ATTN_WIDTH, ATTN_WIDTH, ATTN_WIDTH)


DILATIONS = tuple(dil for _, dil in DILATED_PATTERNS)
ATTN_TILES = ATTN_WIDTH // LANES


def _in_proj_kernel(x_ref, g_ref, w_ref, mu_ref, mo_ref, gt_ref, su_ref, *rest):
    qkv_refs, scr_ref = rest[:-1], rest[-1]
    tm = x_ref.shape[0]
    xn = _rms(x_ref[...], g_ref[...]).astype(bf16)
    off = 0
    for o_ref, width in zip((mu_ref, mo_ref, gt_ref, su_ref), IN_SEGS[:4]):
        o_ref[...] = _dot(xn, w_ref[:, off:off + width]).astype(o_ref.dtype)
        off += width
    nd = len(DILATIONS)
    for a in range(3):
        val = _dot(xn, w_ref[:, off:off + ATTN_WIDTH])
        off += ATTN_WIDTH
        for c in range(ATTN_TILES):
            scr_ref[a, c] = val[:, c * LANES:(c + 1) * LANES]
        for o_ref, dil in zip(qkv_refs[a * nd:(a + 1) * nd], DILATIONS):
            if dil == 1:
                o_ref[...] = val.astype(o_ref.dtype)
                continue
            for r in range(dil):
                for c in range(ATTN_TILES):
                    col = r * ATTN_WIDTH + c * LANES
                    o_ref[:, col:col + LANES] = scr_ref[a, c, pl.ds(r, tm // dil, stride=dil), :].astype(o_ref.dtype)


def _in_proj(x2, gain, w_cat):
    t, d = x2.shape
    tm = min(TOKEN_BLOCK, t)
    widths = IN_SEGS[:4]
    shapes = [(t, w, tm, f32) for w in widths]
    shapes += [(t // dil, dil * ATTN_WIDTH, tm // dil, bf16) for _ in range(3) for dil in DILATIONS]
    return pl.pallas_call(
        _in_proj_kernel,
        grid=(t // tm,),
        in_specs=[pl.BlockSpec((tm, d), lambda i: (i, 0)), _full((1, d)), _full(w_cat.shape)],
        out_specs=[pl.BlockSpec((rows, w), lambda i: (i, 0)) for _, w, rows, _ in shapes],
        out_shape=[jax.ShapeDtypeStruct((n, w), dt) for n, w, _, dt in shapes],
        scratch_shapes=[pltpu.VMEM((3, ATTN_TILES, tm, LANES), f32)],
        compiler_params=_cparams("parallel"),
        name="in_proj",
    )(x2, gain, w_cat)


def _pad_heads(w, axis):
    shape = w.shape
    h = shape[axis] // HEAD_DIM
    w = w.reshape(shape[:axis] + (h, HEAD_DIM) + shape[axis + 1:])
    pad = [(0, 0)] * w.ndim
    pad[axis + 1] = (0, LANES - HEAD_DIM)
    w = jnp.pad(w, pad)
    return w.reshape(shape[:axis] + (h * LANES,) + shape[axis + 1:])


def _block_diag_heads(w):
    h = w.shape[0]
    wp = jnp.pad(w, ((0, 0), (0, LANES - HEAD_DIM), (0, LANES - HEAD_DIM)))
    eye = jnp.eye(h, dtype=w.dtype)
    return (wp[:, :, None, :] * eye[:, None, :, None]).reshape(h * LANES, h * LANES)


def _log_sigmoid(x):
    return jnp.minimum(x, 0.0) - jnp.log1p(jnp.exp(-jnp.abs(x)))


def _mlstm_kernel(mu_ref, mo_ref, gc_ref, gr_ref, cw_ref, cb_ref, wq_ref, wk_ref, wv_ref,
                  gbr_ref, gbc_ref, ng_ref, sk_ref, y_ref, uext_ref, c_ref, m_ref):
    L = MLSTM_CHUNK
    H = MLSTM_HEADS
    R = mu_ref.shape[0]

    @pl.when(pl.program_id(1) == 0)
    def _():
        uext_ref[0:SUBLANES, :] = jnp.zeros((SUBLANES, MLSTM_PAD), f32)
        c_ref[...] = jnp.zeros_like(c_ref)
        m_ref[...] = jnp.zeros_like(m_ref)

    u = mu_ref[...]
    uext_ref[SUBLANES:SUBLANES + R, :] = u
    acc = jnp.broadcast_to(cb_ref[...], (R, MLSTM_PAD))
    for k in range(CONV_WIDTH):
        start = SUBLANES - (CONV_WIDTH - 1) + k
        acc = acc + cw_ref[k:k + 1, :] * uext_ref[start:start + R, :]
    c = acc * jax.nn.sigmoid(acc)
    uext_ref[0:SUBLANES, :] = u[R - SUBLANES:R, :]

    cb = c.astype(bf16)
    q = _dot(cb, wq_ref[...]).astype(bf16)
    kf = _dot(cb, wk_ref[...])
    vf = _dot(u.astype(bf16), wv_ref[...])

    gcol = gc_ref[...] + gbr_ref[...]
    grow = gr_ref[...] + gbc_ref[...]
    lf_col = _log_sigmoid(gcol)
    lf_row = _log_sigmoid(grow)
    ri = lax.broadcasted_iota(i32, (L, L), 0)
    ci = lax.broadcasted_iota(i32, (L, L), 1)
    causal = ci <= ri
    lower = causal.astype(f32)
    upper = (ri <= ci).astype(f32)
    lane = lax.broadcasted_iota(i32, (L, LANES), 1)
    o_gate = jax.nn.sigmoid(mo_ref[...])
    state_c = [c_ref[h] for h in range(H)]
    state_m = [m_ref[h][0:1, 0:1] for h in range(H)]

    for sub in range(R // L):
        rows = slice(sub * L, (sub + 1) * L)
        b_col = _dot_exact(lower, lf_col[rows, :])
        b_row = _dot_exact(lf_row[:, rows], upper)
        for h in range(H):
            sl = slice(h * LANES, (h + 1) * LANES)
            bc = b_col[:, H + h:H + h + 1]
            br = b_row[H + h:H + h + 1, :]
            lic = gcol[rows, h:h + 1]
            lir = grow[h:h + 1, rows]
            g = bc[L - 1:L, :]
            m_prev = state_m[h]
            c_prev = state_c[h]

            d = jnp.where(causal, bc - br + lir, -jnp.inf)
            m_inter = bc + m_prev
            m_t = jnp.maximum(m_inter, jnp.max(d, axis=-1, keepdims=True))
            qh = q[rows, sl]
            kh = kf[rows, sl]
            v_aug = jnp.where(lane == HEAD_DIM, 1.0, vf[rows, sl]).astype(bf16)
            w_intra = jnp.exp(d - m_t) * _dot_nt(qh, kh.astype(bf16))
            inter = jnp.exp(m_inter - m_t)
            nd = inter * _dot(qh, c_prev.astype(bf16)) + _dot(w_intra.astype(bf16), v_aug)
            den = nd[:, HEAD_DIM:HEAD_DIM + 1]
            hh = nd / jnp.maximum(jnp.abs(den), jnp.exp(-m_t))
            hh = jnp.where(lane < HEAD_DIM, hh, 0.0)
            hn = hh * lax.rsqrt(jnp.sum(hh * hh, axis=-1, keepdims=True) * (1.0 / HEAD_DIM) + RMS_EPS)
            y = o_gate[rows, sl] * (hn * ng_ref[:, sl] + sk_ref[:, sl] * c[rows, sl])
            y_ref[rows, sl] = y.astype(y_ref.dtype)

            a = g - bc + lic
            m_new = jnp.maximum(g + m_prev, jnp.max(a, axis=0, keepdims=True))
            decay = jnp.exp(g + m_prev - m_new)
            kw = (kh * jnp.exp(a - m_new)).astype(bf16)
            state_c[h] = decay * c_prev + _dot_tn(kw, v_aug)
            state_m[h] = m_new

    for h in range(H):
        c_ref[h] = state_c[h]
        m_ref[h] = jnp.broadcast_to(state_m[h], (SUBLANES, LANES))


def _mlstm(mu, mo, gates, p, batch):
    t = mu.shape[0]
    s = t // batch
    R = MLSTM_CHUNK * MLSTM_STEP_CHUNKS
    nc = s // R
    g_row = jnp.transpose(gates.reshape(batch, s, LANES)[:, :, :SUBLANES], (0, 2, 1))
    tok = lambda b, c: (b * nc + c, 0)
    w_spec = _full((MLSTM_PAD, MLSTM_PAD))
    v_spec = _full((1, MLSTM_PAD))
    return pl.pallas_call(
        _mlstm_kernel,
        grid=(batch, nc),
        in_specs=[pl.BlockSpec((R, MLSTM_PAD), tok), pl.BlockSpec((R, MLSTM_PAD), tok),
                  pl.BlockSpec((R, LANES), tok),
                  pl.BlockSpec((None, SUBLANES, R), lambda b, c: (b, 0, c)),
                  _full((CONV_WIDTH, MLSTM_PAD)), v_spec, w_spec, w_spec, w_spec,
                  _full((1, LANES)), _full((SUBLANES, 1)), v_spec, v_spec],
        out_specs=pl.BlockSpec((R, MLSTM_PAD), tok),
        out_shape=jax.ShapeDtypeStruct((t, MLSTM_PAD), bf16),
        scratch_shapes=[pltpu.VMEM((R + SUBLANES, MLSTM_PAD), f32),
                        pltpu.VMEM((MLSTM_HEADS, LANES, LANES), f32),
                        pltpu.VMEM((MLSTM_HEADS, SUBLANES, LANES), f32)],
        compiler_params=_cparams("arbitrary", "arbitrary"),
        name="mlstm",
    )(mu, mo, gates, g_row, p["conv_w"], p["conv_b"], p["wq"], p["wk"], p["wv"],
      p["gb_row"], p["gb_col"], p["norm"], p["skip"])


def _s5_param_kernel(are_ref, aim_ref, ldt_ref, bre_ref, bim_ref, pwr_ref, pwi_ref, bbr_ref, bbi_ref):
    a_re = are_ref[...]
    a_im = aim_ref[...]
    dt = jnp.exp(ldt_ref[...])
    mag = jnp.exp(a_re * dt)
    ab_re = mag * jnp.cos(a_im * dt)
    ab_im = mag * jnp.sin(a_im * dt)
    den = a_re * a_re + a_im * a_im
    xr, yi = ab_re - 1.0, ab_im
    coef_re = (xr * a_re + yi * a_im) / den
    coef_im = (yi * a_re - xr * a_im) / den
    b_re = bre_ref[...]
    b_im = bim_ref[...]
    bbr_ref[...] = coef_re[:, None, :] * b_re - coef_im[:, None, :] * b_im
    bbi_ref[...] = coef_re[:, None, :] * b_im + coef_im[:, None, :] * b_re
    pr, pi = ab_re, ab_im
    pwr_ref[0] = pr
    pwi_ref[0] = pi
    for k in range(1, SUBLANES):
        pr, pi = pr * ab_re - pi * ab_im, pr * ab_im + pi * ab_re
        pwr_ref[k] = pr
        pwi_ref[k] = pi


def _s5_params(a_re, a_im, log_dt, b_re, b_im):
    G, P, Hc = S5_GROUPS, S5_STATE, S5_GROUP_CH
    bt_re = jnp.transpose(b_re, (0, 2, 1))
    bt_im = jnp.transpose(b_im, (0, 2, 1))
    return pl.pallas_call(
        _s5_param_kernel,
        out_shape=[jax.ShapeDtypeStruct((SUBLANES, G, P), f32), jax.ShapeDtypeStruct((SUBLANES, G, P), f32),
                   jax.ShapeDtypeStruct((G, Hc, P), f32), jax.ShapeDtypeStruct((G, Hc, P), f32)],
        name="s5_params",
    )(a_re, a_im, log_dt.reshape(G, 1), bt_re, bt_im)


def _gelu_tanh(x):
    return 0.5 * x * (1.0 + jnp.tanh(math.sqrt(2.0 / math.pi) * (x + 0.044715 * (x * x * x))))


def _s5_kernel(u_ref, wb_ref, pwr_ref, pwi_ref, wc_ref, dsk_ref, wg_ref, bg_ref, ng_ref, y_ref,
               sr_ref, si_ref, cr_ref, ci_ref):
    TT = u_ref.shape[0]
    NL = S5_LANES
    ng8 = TT // SUBLANES

    @pl.when(pl.program_id(1) == 0)
    def _():
        cr_ref[...] = jnp.zeros_like(cr_ref)
        ci_ref[...] = jnp.zeros_like(ci_ref)

    u = u_ref[...]
    bu = _dot(u.astype(bf16), wb_ref[...])
    xr = bu[:, :NL].reshape(ng8, SUBLANES, NL)
    xi = bu[:, NL:].reshape(ng8, SUBLANES, NL)
    row = lax.broadcasted_iota(i32, (1, SUBLANES, NL), 1)
    for sh in (1, 2, 4):
        ar = pwr_ref[sh - 1:sh, :][None]
        ai = pwi_ref[sh - 1:sh, :][None]
        tr = pltpu.roll(xr, sh, 1)
        ti = pltpu.roll(xi, sh, 1)
        keep = row >= sh
        xr, xi = (xr + jnp.where(keep, ar * tr - ai * ti, 0.0),
                  xi + jnp.where(keep, ar * ti + ai * tr, 0.0))
    sr_ref[...] = xr.reshape(TT, NL)
    si_ref[...] = xi.reshape(TT, NL)

    pr = pwr_ref[...]
    pi = pwi_ref[...]

    def group(gidx, carry):
        c_re, c_im = carry
        r0 = pl.multiple_of(gidx * SUBLANES, SUBLANES)
        t_re = sr_ref[pl.ds(r0, SUBLANES), :] + pr * c_re - pi * c_im
        t_im = si_ref[pl.ds(r0, SUBLANES), :] + pr * c_im + pi * c_re
        sr_ref[pl.ds(r0, SUBLANES), :] = t_re
        si_ref[pl.ds(r0, SUBLANES), :] = t_im
        return t_re[SUBLANES - 1:SUBLANES, :], t_im[SUBLANES - 1:SUBLANES, :]

    c_re, c_im = lax.fori_loop(0, ng8, group, (cr_ref[...], ci_ref[...]))
    cr_ref[...] = c_re
    ci_ref[...] = c_im

    y = (_dot(sr_ref[...].astype(bf16), wc_ref[0:NL, :]) + _dot(si_ref[...].astype(bf16), wc_ref[NL:2 * NL, :])
         + dsk_ref[...] * u)
    z = _gelu_tanh(y)
    out = z * jax.nn.sigmoid(_dot(z.astype(bf16), wg_ref[...]) + bg_ref[...])
    y_ref[...] = _rms(out, ng_ref[...]).astype(y_ref.dtype)


def _s5(su, p, batch):
    t = su.shape[0]
    s = t // batch
    tt = min(S5_BLOCK, s)
    nb = s // tt
    W, NL = S5_WIDTH, S5_LANES
    tok = lambda b, c: (b * nb + c, 0)
    return pl.pallas_call(
        _s5_kernel,
        grid=(batch, nb),
        in_specs=[pl.BlockSpec((tt, W), tok), _full((W, 2 * NL)), _full((SUBLANES, NL)), _full((SUBLANES, NL)),
                  _full((2 * NL, W)), _full((1, W)), _full((W, W)), _full((1, W)), _full((1, W))],
        out_specs=pl.BlockSpec((tt, W), tok),
        out_shape=jax.ShapeDtypeStruct((t, W), bf16),
        scratch_shapes=[pltpu.VMEM((tt, NL), f32), pltpu.VMEM((tt, NL), f32),
                        pltpu.VMEM((1, NL), f32), pltpu.VMEM((1, NL), f32)],
        compiler_params=_cparams("arbitrary", "arbitrary"),
        name="s5",
    )(su, p["wb"], p["pw_re"], p["pw_im"], p["wc"], p["d"], p["w_glu"], p["b_glu"], p["norm"])


def _t5_bucket(dist):
    exact = dist < REL_MAX_EXACT
    large = REL_MAX_EXACT + (np.log(np.maximum(dist, 1) / REL_MAX_EXACT)
                             / np.log(REL_MAX_DIST / REL_MAX_EXACT)
                             * (REL_BUCKETS - REL_MAX_EXACT)).astype(np.int32)
    large = np.minimum(large, REL_BUCKETS - 1)
    return np.where(exact, dist, large).astype(np.int32)


def _band_offsets():
    i = np.arange(ATTN_BLOCK)[:, None]
    j = np.arange(2 * ATTN_BLOCK)[None, :]
    return ATTN_BLOCK + i - j


def _attn_bias(rel_table, dil):
    bucket = _t5_bucket(np.clip(_band_offsets(), 0, None) * dil)
    pick = (bucket.reshape(-1, 1) == np.arange(REL_BUCKETS)[None, :]).astype(np.float32)
    bias = jnp.dot(pick, rel_table.astype(f32), precision=lax.Precision.HIGHEST)
    return jnp.transpose(bias.reshape(ATTN_BLOCK, 2 * ATTN_BLOCK, ATTN_HEADS), (2, 0, 1))


def _attn_kernel(q_ref, kc_ref, kp_ref, vc_ref, vp_ref, bias_ref, o_ref, lse_ref, kcat_ref, vcat_ref,
                 s_ref, p_ref, *, n_back):
    RB = q_ref.shape[0]
    BLK = ATTN_BLOCK
    n = pl.program_id(2)
    kcat_ref[0:BLK, :] = kp_ref[...]
    kcat_ref[BLK:BLK + RB, :] = kc_ref[...]
    vcat_ref[0:BLK, :] = vp_ref[...]
    vcat_ref[BLK:BLK + RB, :] = vc_ref[...]

    ri = lax.broadcasted_iota(i32, (BLK, 2 * BLK), 0)
    ci = lax.broadcasted_iota(i32, (BLK, 2 * BLK), 1)
    off = BLK + ri - ci
    band = (off >= 0) & (off <= n_back)
    lane_q = lax.broadcasted_iota(i32, (BLK, LANES), 1)
    lane_kv = lax.broadcasted_iota(i32, (2 * BLK, LANES), 1)

    upper_q = lane_q >= HEAD_DIM
    upper_kv = lane_kv >= HEAD_DIM
    pairs = ATTN_HEADS // 2

    def sub_block(j, _):
        r0 = pl.multiple_of(j * BLK, BLK)
        has_prev = (n > 0) | (j > 0)
        valid = band & (has_prev | (ci >= BLK))
        for hp in range(pairs):
            cs = slice(hp * LANES, (hp + 1) * LANES)
            qp = q_ref[pl.ds(r0, BLK), cs]
            kp = kcat_ref[pl.ds(r0, 2 * BLK), cs]
            for e in range(2):
                qm = jnp.where(upper_q == (e == 1), qp, jnp.zeros_like(qp))
                s = _dot_nt(qm, kp) + bias_ref[2 * hp + e]
                s_ref[2 * hp + e] = jnp.where(valid, s, NEG_INF)
        ms = [jnp.max(s_ref[h], axis=-1, keepdims=True) for h in range(ATTN_HEADS)]
        for h in range(ATTN_HEADS):
            p_ref[h] = jnp.exp(s_ref[h] - ms[h]).astype(bf16)
        for hp in range(pairs):
            cs = slice(hp * LANES, (hp + 1) * LANES)
            vp = vcat_ref[pl.ds(r0, 2 * BLK), cs]
            o_pair = None
            lse_pair = None
            for e in range(2):
                mine = upper_kv == (e == 1)
                ones_lane = (1 - e) * HEAD_DIM
                vm = jnp.where(mine, vp, jnp.where(lane_kv == ones_lane, 1.0, 0.0).astype(bf16))
                o_e = _dot(p_ref[2 * hp + e], vm)
                l = o_e[:, ones_lane:ones_lane + 1]
                o_e = o_e / l
                lse_e = jnp.broadcast_to(ms[2 * hp + e] + jnp.log(l), (BLK, LANES))
                o_pair = o_e if o_pair is None else jnp.where(upper_q, o_e, o_pair)
                lse_pair = lse_e if lse_pair is None else jnp.where(upper_q, lse_e, lse_pair)
            o_ref[pl.ds(r0, BLK), cs] = o_pair
            lse_ref[pl.ds(r0, BLK), cs] = lse_pair
        return 0

    lax.fori_loop(0, RB // BLK, sub_block, 0)


def _dilated_attn(q, k, v, bias, window, dil, batch):
    w = ATTN_WIDTH
    n = q.shape[0] // batch
    rb = min(TOKEN_BLOCK, n)
    nblk = n // rb
    per = rb // ATTN_BLOCK
    view = lambda a: a.reshape(batch, n, dil * w)
    cur = pl.BlockSpec((None, rb, w), lambda b, r, i: (b, i, r))
    prev = pl.BlockSpec((None, ATTN_BLOCK, w), lambda b, r, i: (b, jnp.maximum(i * per - 1, 0), r))
    out = pl.BlockSpec((None, rb, w), lambda b, r, i: (b, i, r))
    o, lse = pl.pallas_call(
        functools.partial(_attn_kernel, n_back=window // dil),
        grid=(batch, dil, nblk),
        in_specs=[cur, cur, prev, cur, prev, _full(bias.shape)],
        out_specs=[out, out],
        out_shape=[jax.ShapeDtypeStruct((batch, n, dil * w), f32)] * 2,
        scratch_shapes=[pltpu.VMEM((rb + ATTN_BLOCK, w), bf16), pltpu.VMEM((rb + ATTN_BLOCK, w), bf16),
                        pltpu.VMEM((ATTN_HEADS, ATTN_BLOCK, 2 * ATTN_BLOCK), f32),
                        pltpu.VMEM((ATTN_HEADS, ATTN_BLOCK, 2 * ATTN_BLOCK), bf16)],
        compiler_params=_cparams("parallel", "parallel", "arbitrary"),
        name=f"dilated_attn_{dil}",
    )(view(q), view(k), view(k), view(v), view(v), bias)
    return o.reshape(batch * n, dil * w), lse.reshape(batch * n, dil * w)


def _token_order(ref, scr_ref, slot, dil, tm):
    if dil == 1:
        return ref[...]
    for r in range(dil):
        for c in range(ATTN_TILES):
            col = r * ATTN_WIDTH + c * LANES
            scr_ref[slot, c, pl.ds(r, tm // dil, stride=dil), :] = ref[:, col:col + LANES]
    return jnp.concatenate([scr_ref[slot, c] for c in range(ATTN_TILES)], axis=1)


def _out_proj_kernel(x_ref, ym_ref, ys_ref, o1_ref, o2_ref, o3_ref, l1_ref, l2_ref, l3_ref, ag_ref,
                     wm_ref, ws_ref, wa_ref, out_ref, scr_ref):
    tm = x_ref.shape[0]
    o_refs, l_refs = (o1_ref, o2_ref, o3_ref), (l1_ref, l2_ref, l3_ref)
    o = [_token_order(r, scr_ref, 2 * i, dil, tm) for i, (r, dil) in enumerate(zip(o_refs, DILATIONS))]
    lse = [_token_order(r, scr_ref, 2 * i + 1, dil, tm) for i, (r, dil) in enumerate(zip(l_refs, DILATIONS))]
    mx = jnp.maximum(jnp.maximum(lse[0], lse[1]), lse[2])
    e = [jnp.exp(l - mx) for l in lse]
    tot = e[0] + e[1] + e[2]
    ya = (e[0] / tot) * o[0] + (e[1] / tot) * o[1] + (e[2] / tot) * o[2]
    ya = _rms(ya, ag_ref[...]).astype(bf16)
    out_ref[...] = (x_ref[...] + _dot(ym_ref[...], wm_ref[...]) + _dot(ys_ref[...], ws_ref[...])
                    + _dot(ya, wa_ref[...]))


def _out_proj(x2, ym, ys, attn, a_gain, wm, ws, wa):
    t, d = x2.shape
    tm = min(TOKEN_BLOCK, t)
    row = lambda w: pl.BlockSpec((tm, w), lambda i: (i, 0))
    dilated = [pl.BlockSpec((tm // dil, dil * ATTN_WIDTH), lambda i: (i, 0)) for dil in DILATIONS]
    (o1, l1), (o2, l2), (o3, l3) = attn
    return pl.pallas_call(
        _out_proj_kernel,
        grid=(t // tm,),
        in_specs=[row(d), row(MLSTM_PAD), row(S5_WIDTH)] + dilated + dilated
                 + [_full((1, ATTN_WIDTH)), _full(wm.shape), _full(ws.shape), _full(wa.shape)],
        out_specs=row(d),
        out_shape=jax.ShapeDtypeStruct((t, d), f32),
        scratch_shapes=[pltpu.VMEM((2 * len(DILATIONS), ATTN_TILES, tm, LANES), f32)],
        compiler_params=_cparams("parallel"),
        name="out_proj",
    )(x2, ym, ys, o1, o2, o3, l1, l2, l3, a_gain, wm, ws, wa)


def _mem_kv_kernel(mem_ref, g_ref, w_ref, kv_ref):
    kv_ref[...] = _dot(_rms(mem_ref[...], g_ref[...]).astype(bf16), w_ref[...]).astype(kv_ref.dtype)


def _mem_kv(mem2, gain, w_kv):
    r, d = mem2.shape
    return pl.pallas_call(
        _mem_kv_kernel,
        out_shape=jax.ShapeDtypeStruct((r, 2 * d), bf16),
        compiler_params=pltpu.CompilerParams(vmem_limit_bytes=VMEM_LIMIT),
        name="mem_kv",
    )(mem2, gain, w_kv)


def _xattn_kernel(x_ref, g_ref, wq_ref, kv_ref, wo_ref, out_ref):
    x = x_ref[...]
    d = x.shape[1]
    hd = d // MEM_HEADS
    q = _dot(_rms(x, g_ref[...]).astype(bf16), wq_ref[...]).astype(bf16)
    heads = []
    for h in range(MEM_HEADS):
        kh = kv_ref[:, h * hd:(h + 1) * hd]
        vh = kv_ref[:, d + h * hd:d + (h + 1) * hd]
        s = _dot_nt(q[:, h * hd:(h + 1) * hd], kh) * (hd ** -0.5)
        m = jnp.max(s, axis=-1, keepdims=True)
        p = jnp.exp(s - m)
        p = p / jnp.sum(p, axis=-1, keepdims=True)
        heads.append(_dot(p.astype(bf16), vh).astype(bf16))
    o = jnp.concatenate(heads, axis=-1)
    out_ref[...] = x + _dot(o, wo_ref[...])


def _xattn(x2, gain, w_q, kv, w_o, batch):
    t, d = x2.shape
    s = t // batch
    m = kv.shape[0] // batch
    tm = min(TOKEN_BLOCK, s)
    nb = s // tm
    tok = lambda b, i: (b * nb + i, 0)
    return pl.pallas_call(
        _xattn_kernel,
        grid=(batch, nb),
        in_specs=[pl.BlockSpec((tm, d), tok), _full((1, d)), _full((d, d)),
                  pl.BlockSpec((m, 2 * d), lambda b, i: (b, 0)), _full((d, d))],
        out_specs=pl.BlockSpec((tm, d), tok),
        out_shape=jax.ShapeDtypeStruct((t, d), f32),
        compiler_params=_cparams("parallel", "parallel"),
        name="xattn",
    )(x2, gain, w_q, kv, w_o)


MOE_UNIT = SUBLANES
MOE_BLOCK_UNITS = MOE_ROWS // MOE_UNIT
GATE_PIECES = 3


def _moe_slots(tm):
    worst = 2 * tm + N_EXPERTS * (MOE_UNIT - 1)
    return -(-worst // LANES) * LANES


def _first_argmax(x, lane):
    m = jnp.max(x, axis=-1, keepdims=True)
    idx = jnp.min(jnp.where(x == m, lane, LANES), axis=-1, keepdims=True)
    return m, idx


def _router_kernel(x_ref, g_ref, wr_ref, br_ref, xn_ref, sel_ref, gate_ref, cnt_ref):
    TM = x_ref.shape[0]
    G, EPG, E = MOE_GROUPS, EXPERTS_PER_GROUP, N_EXPERTS
    xn = _rms(x_ref[...], g_ref[...])
    xn_ref[...] = xn.astype(xn_ref.dtype)
    logits = _dot_exact(xn, wr_ref[...]) + br_ref[...]
    lane = lax.broadcasted_iota(i32, (TM, LANES), 1)
    is_group = (lane >= E) & (lane < E + G)
    gl = jnp.where(is_group, logits, -jnp.inf)
    gmax, gidx = _first_argmax(gl, lane)
    g_p = 1.0 / jnp.sum(jnp.exp(gl - gmax), axis=-1, keepdims=True)
    g_i = gidx - E
    in_group = (lane < E) & ((lane >> int(math.log2(EPG))) == g_i)
    el = jnp.where(in_group, logits, -jnp.inf)
    m1, i1 = _first_argmax(el, lane)
    m2, i2 = _first_argmax(jnp.where(lane == i1, -jnp.inf, el), lane)
    p2 = jnp.exp(m2 - m1)
    gate1 = g_p / (1.0 + p2)
    gate2 = g_p * p2 / (1.0 + p2)

    onehot = ((lane == i1) | (lane == i2)).astype(bf16)
    r = lax.broadcasted_iota(i32, (TM, TM), 0)
    c = lax.broadcasted_iota(i32, (TM, TM), 1)
    before = _dot((c < r).astype(bf16), onehot)
    cnt = jnp.sum(onehot.astype(f32), axis=0, keepdims=True)
    padded = jnp.floor((cnt + (MOE_UNIT - 1)) * (1.0 / MOE_UNIT)) * MOE_UNIT
    er = lax.broadcasted_iota(i32, (LANES, LANES), 0)
    ec = lax.broadcasted_iota(i32, (LANES, LANES), 1)
    padded8 = jnp.broadcast_to(padded, (SUBLANES, LANES))
    seg_start = _dot_exact(padded8, (er < ec).astype(f32))[0:1, :]
    slot = seg_start + before
    pos1 = jnp.sum(jnp.where(lane == i1, slot, 0.0), axis=-1, keepdims=True)
    pos2 = jnp.sum(jnp.where(lane == i2, slot, 0.0), axis=-1, keepdims=True)
    sel_ref[...] = jnp.where(lane == 0, pos1, jnp.where(lane == 1, pos2, 0.0))
    gate_ref[...] = jnp.where(lane == 0, gate1, jnp.where(lane == 1, gate2, 0.0))
    cnt_ref[...] = padded8.astype(i32)


def _router(x2, gain, w_r, b_r):
    t, d = x2.shape
    tm = min(TOKEN_BLOCK, t)
    nblk = t // tm
    row = lambda w: pl.BlockSpec((tm, w), lambda i: (i, 0))
    return pl.pallas_call(
        _router_kernel,
        grid=(nblk,),
        in_specs=[row(d), _full((1, d)), _full((d, LANES)), _full((1, LANES))],
        out_specs=[row(d), row(LANES), row(LANES), pl.BlockSpec((None, SUBLANES, LANES), lambda i: (i, 0, 0))],
        out_shape=[jax.ShapeDtypeStruct((t, d), bf16), jax.ShapeDtypeStruct((t, LANES), f32),
                   jax.ShapeDtypeStruct((t, LANES), f32), jax.ShapeDtypeStruct((nblk, SUBLANES, LANES), i32)],
        compiler_params=_cparams("parallel"),
        name="moe_router",
    )(x2, gain, w_r, b_r)


def _unit_copy(src_ref, src_unit, dst_ref, dst_unit, sem):
    src = src_ref.at[pl.ds(pl.multiple_of(src_unit * MOE_UNIT, MOE_UNIT), MOE_UNIT), :]
    dst = dst_ref.at[pl.ds(pl.multiple_of(dst_unit * MOE_UNIT, MOE_UNIT), MOE_UNIT), :]
    return pltpu.make_async_copy(src, dst, sem)


def _gate_pieces(g, lane):
    hi = g.astype(bf16).astype(f32)
    r1 = g - hi
    mid = r1.astype(bf16).astype(f32)
    lo = r1 - mid
    return jnp.where(lane == 0, hi, jnp.where(lane == 1, mid, jnp.where(lane == 2, lo, 0.0))).astype(bf16)


def _dispatch_kernel(unit_ref, tail_ref, xn_ref, sel_ref, gate_ref, rows_ref, sorted_ref, zero_ref, sems):
    TM, D = xn_ref.shape
    NS = sorted_ref.shape[1]
    NU = NS // MOE_UNIT
    tb = pl.program_id(0)
    sem = sems.at[0]

    @pl.when(tb == 0)
    def _():
        zero_ref[...] = jnp.zeros_like(zero_ref)

        def zero_copy(e):
            start = pl.multiple_of(tail_ref[e] * MOE_UNIT - MOE_ROWS, MOE_UNIT)
            return pltpu.make_async_copy(zero_ref, rows_ref.at[pl.ds(start, MOE_ROWS), :], sem)

        def issue(e, _):
            @pl.when(tail_ref[N_EXPERTS + e] > 0)
            def _():
                zero_copy(e).start()
            return 0

        def drain(e, _):
            @pl.when(tail_ref[N_EXPERTS + e] > 0)
            def _():
                zero_copy(e).wait()
            return 0

        lax.fori_loop(0, N_EXPERTS, issue, 0)
        lax.fori_loop(0, N_EXPERTS, drain, 0)

        def block_copy(i):
            start = pl.multiple_of(i * MOE_ROWS, MOE_ROWS)
            return pltpu.make_async_copy(zero_ref, rows_ref.at[pl.ds(start, MOE_ROWS), :], sem)

        def issue_block(i, _):
            block_copy(i).start()
            return 0

        def drain_block(i, _):
            block_copy(i).wait()
            return 0

        n_used = tail_ref[2 * N_EXPERTS]
        lax.fori_loop(n_used, rows_ref.shape[0] // MOE_ROWS, issue_block, 0)
        lax.fori_loop(n_used, rows_ref.shape[0] // MOE_ROWS, drain_block, 0)

    sel_t = jnp.transpose(sel_ref[...])
    slot = lax.broadcasted_iota(i32, (NS, TM), 0).astype(f32)
    p1 = jnp.where(slot == sel_t[0:1, :], 1.0, 0.0).astype(bf16)
    p2 = jnp.where(slot == sel_t[1:2, :], 1.0, 0.0).astype(bf16)
    lane = lax.broadcasted_iota(i32, (TM, LANES), 1)
    gates = gate_ref[...]
    cur = lax.rem(tb, 2)
    sorted_ref[cur, :, 0:D] = _dot(p1 + p2, xn_ref[...])
    sorted_ref[cur, :, D:D + LANES] = (_dot(p1, _gate_pieces(gates[:, 0:1], lane))
                                       + _dot(p2, _gate_pieces(gates[:, 1:2], lane)))

    def for_units(block, buf, action):
        def body(j, _):
            u = unit_ref[block * NU + j]

            @pl.when(u >= 0)
            def _():
                action(_unit_copy(sorted_ref.at[buf], j, rows_ref, u, sems.at[buf]))
            return 0

        lax.fori_loop(0, NU, body, 0)

    @pl.when(tb > 0)
    def _():
        for_units(tb - 1, 1 - cur, lambda c: c.wait())

    for_units(tb, cur, lambda c: c.start())

    @pl.when(tb == pl.num_programs(0) - 1)
    def _():
        for_units(tb, cur, lambda c: c.wait())


def _dispatch(unit_map, tail, xn, sel, gates, n_rows):
    t, d = xn.shape
    tm = min(TOKEN_BLOCK, t)
    ns = _moe_slots(tm)
    row = lambda w: pl.BlockSpec((tm, w), lambda i, um, tl: (i, 0))
    return pl.pallas_call(
        _dispatch_kernel,
        grid_spec=pltpu.PrefetchScalarGridSpec(
            num_scalar_prefetch=2,
            grid=(t // tm,),
            in_specs=[row(d), row(LANES), row(LANES)],
            out_specs=pl.BlockSpec(memory_space=pl.ANY),
            scratch_shapes=[pltpu.VMEM((2, ns, d + LANES), f32), pltpu.VMEM((MOE_ROWS, d + LANES), f32),
                            pltpu.SemaphoreType.DMA((2,))],
        ),
        out_shape=jax.ShapeDtypeStruct((n_rows, d + LANES), f32),
        compiler_params=_cparams("arbitrary"),
        name="moe_dispatch",
    )(unit_map, tail, xn, sel, gates)


def _expert_kernel(be_ref, nu_ref, x_ref, wgu_ref, wd_ref, y_ref, wgu_bf, wd_bf):
    i = pl.program_id(0)
    de, d = wd_ref.shape

    @pl.when((i == 0) | (be_ref[i] != be_ref[jnp.maximum(i - 1, 0)]))
    def _():
        wgu_bf[...] = wgu_ref[...].astype(bf16)
        wd_bf[...] = wd_ref[...].astype(bf16)

    @pl.when(i < nu_ref[0])
    def _():
        gate = x_ref[:, d:d + 1]
        for k in range(1, GATE_PIECES):
            gate = gate + x_ref[:, d + k:d + k + 1]
        ab = _dot(x_ref[:, 0:d].astype(bf16), wgu_bf[...])
        a, b = ab[:, :de], ab[:, de:]
        hid = (a * jax.nn.sigmoid(a) * b).astype(bf16)
        y_ref[...] = _dot(hid, wd_bf[...]) * gate

    @pl.when(i >= nu_ref[0])
    def _():
        y_ref[...] = jnp.zeros_like(y_ref)


def _experts(block_e, n_used, rows, w_gu, w_down):
    n_rows, width = rows.shape
    nblk = n_rows // MOE_ROWS
    de, d = w_down.shape[1:]
    used = lambda i, nu: jnp.maximum(jnp.minimum(i, nu[0] - 1), 0)
    return pl.pallas_call(
        _expert_kernel,
        grid_spec=pltpu.PrefetchScalarGridSpec(
            num_scalar_prefetch=2,
            grid=(nblk,),
            in_specs=[pl.BlockSpec((MOE_ROWS, width), lambda i, be, nu: (used(i, nu), 0)),
                      pl.BlockSpec((None, d, 2 * de), lambda i, be, nu: (be[i], 0, 0)),
                      pl.BlockSpec((None, de, d), lambda i, be, nu: (be[i], 0, 0))],
            out_specs=pl.BlockSpec((MOE_ROWS, d), lambda i, be, nu: (i, 0)),
            scratch_shapes=[pltpu.VMEM((d, 2 * de), bf16), pltpu.VMEM((de, d), bf16)],
        ),
        out_shape=jax.ShapeDtypeStruct((n_rows, d), f32),
        compiler_params=_cparams("arbitrary"),
        name="moe_experts",
    )(block_e, n_used, rows, w_gu, w_down)


def _combine_kernel(unit_ref, x_ref, sel_ref, fg_ref, y_hbm, out_ref, ybuf, sems, *, final_norm):
    TM = x_ref.shape[0]
    NS, D = ybuf.shape[1:]
    NU = NS // MOE_UNIT
    tb = pl.program_id(0)
    cur = lax.rem(tb, 2)

    def fetch(block, buf):
        def body(j, _):
            u = unit_ref[block * NU + j]

            @pl.when(u >= 0)
            def _():
                _unit_copy(y_hbm, u, ybuf.at[buf], j, sems.at[buf]).start()

            @pl.when(u < 0)
            def _():
                ybuf[buf, pl.ds(pl.multiple_of(j * MOE_UNIT, MOE_UNIT), MOE_UNIT), :] = jnp.zeros((MOE_UNIT, D), f32)
            return 0

        lax.fori_loop(0, NU, body, 0)

    def drain(block, buf):
        def body(j, _):
            u = unit_ref[block * NU + j]

            @pl.when(u >= 0)
            def _():
                _unit_copy(y_hbm, u, ybuf.at[buf], j, sems.at[buf]).wait()
            return 0

        lax.fori_loop(0, NU, body, 0)

    @pl.when(tb == 0)
    def _():
        fetch(tb, cur)

    @pl.when(tb + 1 < pl.num_programs(0))
    def _():
        fetch(tb + 1, 1 - cur)

    drain(tb, cur)

    sel = sel_ref[...]
    slot = lax.broadcasted_iota(i32, (TM, NS), 1).astype(f32)
    pick = jnp.where((slot == sel[:, 0:1]) | (slot == sel[:, 1:2]), 1.0, 0.0).astype(bf16)
    y = ybuf[cur]
    hi = y.astype(bf16)
    lo = (y - hi.astype(f32)).astype(bf16)
    out = x_ref[...] + _dot(pick, hi) + _dot(pick, lo)
    if final_norm:
        out = _rms(out, fg_ref[...])
    out_ref[...] = out


def _combine(unit_map, x2, sel, y_rows, final_gain, final_norm):
    t, d = x2.shape
    tm = min(TOKEN_BLOCK, t)
    ns = _moe_slots(tm)
    return pl.pallas_call(
        functools.partial(_combine_kernel, final_norm=final_norm),
        grid_spec=pltpu.PrefetchScalarGridSpec(
            num_scalar_prefetch=1,
            grid=(t // tm,),
            in_specs=[pl.BlockSpec((tm, d), lambda i, um: (i, 0)),
                      pl.BlockSpec((tm, LANES), lambda i, um: (i, 0)),
                      pl.BlockSpec((1, d), lambda i, um: (0, 0)),
                      pl.BlockSpec(memory_space=pl.ANY)],
            out_specs=pl.BlockSpec((tm, d), lambda i, um: (i, 0)),
            scratch_shapes=[pltpu.VMEM((2, ns, d), f32), pltpu.SemaphoreType.DMA((2,))],
        ),
        out_shape=jax.ShapeDtypeStruct((t, d), f32),
        compiler_params=_cparams("arbitrary"),
        name="moe_combine",
    )(unit_map, x2, sel, final_gain, y_rows)


def _moe(x2, gain, w_r, b_r, w_gu, w_down, final_gain, final_norm):
    t, d = x2.shape
    tm = min(TOKEN_BLOCK, t)
    nblk = t // tm
    nu = _moe_slots(tm) // MOE_UNIT
    xn, sel, gates, cnt = _router(x2, gain, w_r, b_r)
    units = cnt[:, 0, :N_EXPERTS] // MOE_UNIT
    seg_end = jnp.cumsum(units, axis=1)
    seg_start = seg_end - units
    run_off = jnp.cumsum(units, axis=0) - units
    total = jnp.sum(units, axis=0)
    region = (total + MOE_BLOCK_UNITS - 1) // MOE_BLOCK_UNITS * MOE_BLOCK_UNITS
    g_end = jnp.cumsum(region)
    g_start = g_end - region
    j = jnp.arange(nu, dtype=i32)
    owner = jnp.sum((seg_end[:, None, :] <= j[None, :, None]).astype(i32), axis=2)
    hit = owner[:, :, None] == jnp.arange(N_EXPERTS, dtype=i32)[None, None, :]
    shift = (g_start[None, :] + run_off - seg_start)[:, None, :]
    unit_map = jnp.where(owner < N_EXPERTS, j[None, :] + jnp.sum(jnp.where(hit, shift, 0), axis=2), -1)
    unit_map = unit_map.reshape(-1).astype(i32)
    tail = jnp.concatenate([g_end, region - total, g_end[-1:] // MOE_BLOCK_UNITS]).astype(i32)
    n_blocks = -(-(nblk * nu * MOE_UNIT + N_EXPERTS * (MOE_ROWS - MOE_UNIT)) // MOE_ROWS)
    block_start = jnp.arange(n_blocks, dtype=i32) * MOE_BLOCK_UNITS
    block_e = jnp.minimum(jnp.sum((g_end[None, :] <= block_start[:, None]).astype(i32), axis=1), N_EXPERTS - 1)
    n_used = (g_end[-1:] // MOE_BLOCK_UNITS).astype(i32)
    rows = _dispatch(unit_map, tail, xn, sel, gates, n_blocks * MOE_ROWS)
    y_rows = _experts(block_e, n_used, rows, w_gu, w_down)
    return _combine(unit_map, x2, sel, y_rows, final_gain, final_norm)


def _layer_params(l, a):
    H, W = MLSTM_HEADS, MLSTM_WIDTH
    w_in = a["w_in"][l]
    sizes = [W, W, H, H, S5_WIDTH, ATTN_WIDTH, ATTN_WIDTH, ATTN_WIDTH]
    offs = np.concatenate([[0], np.cumsum(sizes)])
    seg = lambda i: w_in[:, offs[i]:offs[i + 1]]
    gates_w = jnp.pad(jnp.concatenate([seg(2), seg(3)], axis=1), ((0, 0), (0, LANES - 2 * H)))
    w_cat = jnp.concatenate([_pad_heads(seg(0), 1), _pad_heads(seg(1), 1), gates_w, seg(4),
                             seg(5) * (HEAD_DIM ** -0.5), seg(6), seg(7)], axis=1).astype(bf16)
    row = lambda v: v.reshape(1, -1)
    gb = a["mlstm_gate_bias"][l]
    mlstm = dict(
        conv_w=_pad_heads(a["mlstm_conv_w"][l], 1), conv_b=row(_pad_heads(a["mlstm_conv_b"][l], 0)),
        wq=_block_diag_heads(a["mlstm_wq"][l] * (HEAD_DIM ** -0.5)).astype(bf16),
        wk=_block_diag_heads(a["mlstm_wk"][l]).astype(bf16),
        wv=_block_diag_heads(a["mlstm_wv"][l]).astype(bf16),
        gb_row=jnp.pad(gb, (0, LANES - 2 * H)).reshape(1, LANES), gb_col=gb.reshape(2 * H, 1),
        norm=row(_pad_heads(a["mlstm_norm"][l], 0)), skip=row(_pad_heads(a["mlstm_skip"][l], 0)))

    G, P, Hc = S5_GROUPS, S5_STATE, S5_GROUP_CH
    pw_re, pw_im, bb_re, bb_im = _s5_params(a["s5_a_re"][l], a["s5_a_im"][l], a["s5_log_dt"][l],
                                            a["s5_b_re"][l], a["s5_b_im"][l])
    eye = jnp.eye(G, dtype=f32)
    bd_in = lambda m: (m[:, :, None, :] * eye[:, None, :, None]).reshape(G * Hc, G * P)
    bd_out = lambda m: (jnp.transpose(m, (0, 2, 1))[:, :, None, :] * eye[:, None, :, None]).reshape(G * P, G * Hc)
    s5 = dict(
        wb=jnp.concatenate([bd_in(bb_re), bd_in(bb_im)], axis=1).astype(bf16),
        wc=jnp.concatenate([bd_out(a["s5_c_re"][l]), -bd_out(a["s5_c_im"][l])], axis=0).astype(bf16),
        pw_re=pw_re.reshape(SUBLANES, G * P), pw_im=pw_im.reshape(SUBLANES, G * P),
        d=a["s5_d"][l].reshape(1, S5_WIDTH), w_glu=a["s5_w_glu"][l].astype(bf16),
        b_glu=row(a["s5_b_glu"][l]), norm=row(a["s5_out_norm"][l]))

    w_out = a["w_out"][l]
    w_r = jnp.pad(jnp.concatenate([a["router_w_expert"][l], a["router_w_group"][l]], axis=1),
                  ((0, 0), (0, LANES - N_EXPERTS - MOE_GROUPS)))
    b_r = jnp.pad(jnp.concatenate([a["router_b_expert"][l], a["router_b_group"][l]]),
                  (0, LANES - N_EXPERTS - MOE_GROUPS)).reshape(1, LANES)
    return dict(
        norm_mix=row(a["norm_mix"][l]), w_cat=w_cat, mlstm=mlstm, s5=s5,
        attn_norm=row(a["attn_out_norm"][l]),
        wo_m=_pad_heads(w_out[:W], 0).astype(bf16), wo_s=w_out[W:W + S5_WIDTH].astype(bf16),
        wo_a=w_out[W + S5_WIDTH:].astype(bf16),
        norm_xattn=row(a["norm_xattn"][l]), norm_mem=row(a["norm_mem"][l]),
        xq=a["xattn_w_q"][l].astype(bf16), xkv=a["xattn_w_kv"][l].astype(bf16), xo=a["xattn_w_o"][l].astype(bf16),
        norm_ffn=row(a["norm_ffn"][l]), w_r=w_r, b_r=b_r,
        w_gu=a["expert_w_gu"][l], w_down=a["expert_w_down"][l])


def _forward(a):
    x = a["x"]
    batch, seq, d = x.shape
    depth = a["w_in"].shape[0]
    x2 = x.reshape(batch * seq, d)
    mem2 = a["mem"].reshape(-1, d)
    biases = [_attn_bias(a["rel_bias"], dil) for _, dil in DILATED_PATTERNS]
    final_gain = a["final_norm"].reshape(1, d)
    for l in range(depth):
        p = _layer_params(l, a)
        mu, mo, gates, su, *qkv = _in_proj(x2, p["norm_mix"], p["w_cat"])
        ym = _mlstm(mu, mo, gates, p["mlstm"], batch)
        ys = _s5(su, p["s5"], batch)
        nd = len(DILATIONS)
        attn = [_dilated_attn(qkv[i], qkv[nd + i], qkv[2 * nd + i], bias, window, dil, batch)
                for i, (bias, (window, dil)) in enumerate(zip(biases, DILATED_PATTERNS))]
        x2 = _out_proj(x2, ym, ys, attn, p["attn_norm"], p["wo_m"], p["wo_s"], p["wo_a"])
        kv = _mem_kv(mem2, p["norm_mem"], p["xkv"])
        x2 = _xattn(x2, p["norm_xattn"], p["xq"], kv, p["xo"], batch)
        x2 = _moe(x2, p["norm_ffn"], p["w_r"], p["b_r"], p["w_gu"], p["w_down"], final_gain,
                  final_norm=(l == depth - 1))
    return x2.reshape(batch, seq, d)


def kernel(x, mem, rel_bias, norm_mix, w_in, mlstm_conv_w, mlstm_conv_b, mlstm_wq, mlstm_wk, mlstm_wv,
           mlstm_gate_bias, mlstm_norm, mlstm_skip, s5_a_re, s5_a_im, s5_log_dt, s5_b_re, s5_b_im, s5_c_re,
           s5_c_im, s5_d, s5_w_glu, s5_b_glu, s5_out_norm, attn_out_norm, w_out, norm_xattn, norm_mem,
           xattn_w_q, xattn_w_kv, xattn_w_o, norm_ffn, router_w_group, router_b_group, router_w_expert,
           router_b_expert, expert_w_gu, expert_w_down, final_norm):
    return _forward(dict(locals()))
```

```python
import functools
import math

import numpy as np
import jax
import jax.numpy as jnp
from jax import lax
from jax.experimental import pallas as pl
from jax.experimental.pallas import tpu as pltpu

f32 = jnp.float32
bf16 = jnp.bfloat16
i32 = jnp.int32

HEAD_DIM = 64
MLSTM_HEADS = 4
MLSTM_WIDTH = MLSTM_HEADS * HEAD_DIM
CONV_WIDTH = 4
S5_GROUPS = 16
S5_GROUP_CH = 16
S5_STATE = 64
S5_WIDTH = S5_GROUPS * S5_GROUP_CH
S5_LANES = S5_GROUPS * S5_STATE
ATTN_HEADS = 8
ATTN_WIDTH = ATTN_HEADS * HEAD_DIM
ATTN_BLOCK = 128
DILATED_PATTERNS = ((128, 1), (512, 4), (2048, 16))
REL_BUCKETS = 32
REL_MAX_EXACT = 16
REL_MAX_DIST = 2048
MEM_HEADS = 4
MOE_GROUPS = 4
EXPERTS_PER_GROUP = 8
N_EXPERTS = MOE_GROUPS * EXPERTS_PER_GROUP
RMS_EPS = 1e-6
NEG_INF = -1e30

LANES = 128
SUBLANES = 8
VMEM_LIMIT = 48 * 1024 * 1024

MLSTM_PAD = MLSTM_HEADS * LANES
MLSTM_CHUNK = 128
MLSTM_STEP_CHUNKS = 2
S5_BLOCK = 256
MOE_ROWS = 512
TOKEN_BLOCK = 512


def _cparams(*sem):
    return pltpu.CompilerParams(dimension_semantics=sem, vmem_limit_bytes=VMEM_LIMIT)


def _rms(x, gain):
    return x * lax.rsqrt(jnp.mean(x * x, axis=-1, keepdims=True) + RMS_EPS) * gain


def _dot(a, b):
    return jnp.dot(a, b, preferred_element_type=f32)


def _dot_nt(a, b):
    return lax.dot_general(a, b, (((1,), (1,)), ((), ())), preferred_element_type=f32)


def _dot_tn(a, b):
    return lax.dot_general(a, b, (((0,), (0,)), ((), ())), preferred_element_type=f32)


def _dot_exact(a, b):
    return jnp.dot(a, b, preferred_element_type=f32, precision=lax.Precision.HIGHEST)


def _full(shape):
    n = len(shape)
    return pl.BlockSpec(shape, lambda *_: (0,) * n)


IN_SEGS = (MLSTM_PAD, MLSTM_PAD, LANES, S5_WIDTH, ATTN_WIDTH, ATTN_WIDTH, ATTN_WIDTH)


DILATIONS = tuple(dil for _, dil in DILATED_PATTERNS)
ATTN_TILES = ATTN_WIDTH // LANES


def _in_proj_kernel(x_ref, g_ref, w_ref, mu_ref, mo_ref, gt_ref, su_ref, *rest):
    qkv_refs, scr_ref = rest[:-1], rest[-1]
    tm = x_ref.shape[0]
    xn = _rms(x_ref[...], g_ref[...]).astype(bf16)
    off = 0
    for o_ref, width in zip((mu_ref, mo_ref, gt_ref, su_ref), IN_SEGS[:4]):
        o_ref[...] = _dot(xn, w_ref[:, off:off + width]).astype(o_ref.dtype)
        off += width
    nd = len(DILATIONS)
    for a in range(3):
        val = _dot(xn, w_ref[:, off:off + ATTN_WIDTH])
        off += ATTN_WIDTH
        for c in range(ATTN_TILES):
            scr_ref[a, c] = val[:, c * LANES:(c + 1) * LANES]
        for o_ref, dil in zip(qkv_refs[a * nd:(a + 1) * nd], DILATIONS):
            if dil == 1:
                o_ref[...] = val.astype(o_ref.dtype)
                continue
            for r in range(dil):
                for c in range(ATTN_TILES):
                    col = r * ATTN_WIDTH + c * LANES
                    o_ref[:, col:col + LANES] = scr_ref[a, c, pl.ds(r, tm // dil, stride=dil), :].astype(o_ref.dtype)


def _in_proj(x2, gain, w_cat):
    t, d = x2.shape
    tm = min(TOKEN_BLOCK, t)
    widths = IN_SEGS[:4]
    shapes = [(t, w, tm, f32) for w in widths]
    shapes += [(t // dil, dil * ATTN_WIDTH, tm // dil, bf16) for _ in range(3) for dil in DILATIONS]
    return pl.pallas_call(
        _in_proj_kernel,
        grid=(t // tm,),
        in_specs=[pl.BlockSpec((tm, d), lambda i: (i, 0)), _full((1, d)), _full(w_cat.shape)],
        out_specs=[pl.BlockSpec((rows, w), lambda i: (i, 0)) for _, w, rows, _ in shapes],
        out_shape=[jax.ShapeDtypeStruct((n, w), dt) for n, w, _, dt in shapes],
        scratch_shapes=[pltpu.VMEM((3, ATTN_TILES, tm, LANES), f32)],
        compiler_params=_cparams("parallel"),
        name="in_proj",
    )(x2, gain, w_cat)


def _pad_heads(w, axis):
    shape = w.shape
    h = shape[axis] // HEAD_DIM
    w = w.reshape(shape[:axis] + (h, HEAD_DIM) + shape[axis + 1:])
    pad = [(0, 0)] * w.ndim
    pad[axis + 1] = (0, LANES - HEAD_DIM)
    w = jnp.pad(w, pad)
    return w.reshape(shape[:axis] + (h * LANES,) + shape[axis + 1:])


def _block_diag_heads(w):
    h = w.shape[0]
    wp = jnp.pad(w, ((0, 0), (0, LANES - HEAD_DIM), (0, LANES - HEAD_DIM)))
    eye = jnp.eye(h, dtype=w.dtype)
    return (wp[:, :, None, :] * eye[:, None, :, None]).reshape(h * LANES, h * LANES)


def _log_sigmoid(x):
    return jnp.minimum(x, 0.0) - jnp.log1p(jnp.exp(-jnp.abs(x)))


def _mlstm_kernel(mu_ref, mo_ref, gc_ref, gr_ref, cw_ref, cb_ref, wq_ref, wk_ref, wv_ref,
                  gbr_ref, gbc_ref, ng_ref, sk_ref, y_ref, uext_ref, c_ref, m_ref):
    L = MLSTM_CHUNK
    H = MLSTM_HEADS
    R = mu_ref.shape[0]

    @pl.when(pl.program_id(1) == 0)
    def _():
        uext_ref[0:SUBLANES, :] = jnp.zeros((SUBLANES, MLSTM_PAD), f32)
        c_ref[...] = jnp.zeros_like(c_ref)
        m_ref[...] = jnp.zeros_like(m_ref)

    u = mu_ref[...]
    uext_ref[SUBLANES:SUBLANES + R, :] = u
    acc = jnp.broadcast_to(cb_ref[...], (R, MLSTM_PAD))
    for k in range(CONV_WIDTH):
        start = SUBLANES - (CONV_WIDTH - 1) + k
        acc = acc + cw_ref[k:k + 1, :] * uext_ref[start:start + R, :]
    c = acc * jax.nn.sigmoid(acc)
    uext_ref[0:SUBLANES, :] = u[R - SUBLANES:R, :]

    cb = c.astype(bf16)
    q = _dot(cb, wq_ref[...]).astype(bf16)
    kf = _dot(cb, wk_ref[...])
    vf = _dot(u.astype(bf16), wv_ref[...])

    gcol = gc_ref[...] + gbr_ref[...]
    grow = gr_ref[...] + gbc_ref[...]
    lf_col = _log_sigmoid(gcol)
    lf_row = _log_sigmoid(grow)
    ri = lax.broadcasted_iota(i32, (L, L), 0)
    ci = lax.broadcasted_iota(i32, (L, L), 1)
    causal = ci <= ri
    lower = causal.astype(f32)
    upper = (ri <= ci).astype(f32)
    lane = lax.broadcasted_iota(i32, (L, LANES), 1)
    o_gate = jax.nn.sigmoid(mo_ref[...])
    state_c = [c_ref[h] for h in range(H)]
    state_m = [m_ref[h][0:1, 0:1] for h in range(H)]

    for sub in range(R // L):
        rows = slice(sub * L, (sub + 1) * L)
        b_col = _dot_exact(lower, lf_col[rows, :])
        b_row = _dot_exact(lf_row[:, rows], upper)
        for h in range(H):
            sl = slice(h * LANES, (h + 1) * LANES)
            bc = b_col[:, H + h:H + h + 1]
            br = b_row[H + h:H + h + 1, :]
            lic = gcol[rows, h:h + 1]
            lir = grow[h:h + 1, rows]
            g = bc[L - 1:L, :]
            m_prev = state_m[h]
            c_prev = state_c[h]

            d = jnp.where(causal, bc - br + lir, -jnp.inf)
            m_inter = bc + m_prev
            m_t = jnp.maximum(m_inter, jnp.max(d, axis=-1, keepdims=True))
            qh = q[rows, sl]
            kh = kf[rows, sl]
            v_aug = jnp.where(lane == HEAD_DIM, 1.0, vf[rows, sl]).astype(bf16)
            w_intra = jnp.exp(d - m_t) * _dot_nt(qh, kh.astype(bf16))
            inter = jnp.exp(m_inter - m_t)
            nd = inter * _dot(qh, c_prev.astype(bf16)) + _dot(w_intra.astype(bf16), v_aug)
            den = nd[:, HEAD_DIM:HEAD_DIM + 1]
            hh = nd / jnp.maximum(jnp.abs(den), jnp.exp(-m_t))
            hh = jnp.where(lane < HEAD_DIM, hh, 0.0)
            hn = hh * lax.rsqrt(jnp.sum(hh * hh, axis=-1, keepdims=True) * (1.0 / HEAD_DIM) + RMS_EPS)
            y = o_gate[rows, sl] * (hn * ng_ref[:, sl] + sk_ref[:, sl] * c[rows, sl])
            y_ref[rows, sl] = y.astype(y_ref.dtype)

            a = g - bc + lic
            m_new = jnp.maximum(g + m_prev, jnp.max(a, axis=0, keepdims=True))
            decay = jnp.exp(g + m_prev - m_new)
            kw = (kh * jnp.exp(a - m_new)).astype(bf16)
            state_c[h] = decay * c_prev + _dot_tn(kw, v_aug)
            state_m[h] = m_new

    for h in range(H):
        c_ref[h] = state_c[h]
        m_ref[h] = jnp.broadcast_to(state_m[h], (SUBLANES, LANES))


def _mlstm(mu, mo, gates, p, batch):
    t = mu.shape[0]
    s = t // batch
    R = MLSTM_CHUNK * MLSTM_STEP_CHUNKS
    nc = s // R
    g_row = jnp.transpose(gates.reshape(batch, s, LANES)[:, :, :SUBLANES], (0, 2, 1))
    tok = lambda b, c: (b * nc + c, 0)
    w_spec = _full((MLSTM_PAD, MLSTM_PAD))
    v_spec = _full((1, MLSTM_PAD))
    return pl.pallas_call(
        _mlstm_kernel,
        grid=(batch, nc),
        in_specs=[pl.BlockSpec((R, MLSTM_PAD), tok), pl.BlockSpec((R, MLSTM_PAD), tok),
                  pl.BlockSpec((R, LANES), tok),
                  pl.BlockSpec((None, SUBLANES, R), lambda b, c: (b, 0, c)),
                  _full((CONV_WIDTH, MLSTM_PAD)), v_spec, w_spec, w_spec, w_spec,
                  _full((1, LANES)), _full((SUBLANES, 1)), v_spec, v_spec],
        out_specs=pl.BlockSpec((R, MLSTM_PAD), tok),
        out_shape=jax.ShapeDtypeStruct((t, MLSTM_PAD), bf16),
        scratch_shapes=[pltpu.VMEM((R + SUBLANES, MLSTM_PAD), f32),
                        pltpu.VMEM((MLSTM_HEADS, LANES, LANES), f32),
                        pltpu.VMEM((MLSTM_HEADS, SUBLANES, LANES), f32)],
        compiler_params=_cparams("arbitrary", "arbitrary"),
        name="mlstm",
    )(mu, mo, gates, g_row, p["conv_w"], p["conv_b"], p["wq"], p["wk"], p["wv"],
      p["gb_row"], p["gb_col"], p["norm"], p["skip"])


def _s5_param_kernel(are_ref, aim_ref, ldt_ref, bre_ref, bim_ref, pwr_ref, pwi_ref, bbr_ref, bbi_ref):
    a_re = are_ref[...]
    a_im = aim_ref[...]
    dt = jnp.exp(ldt_ref[...])
    mag = jnp.exp(a_re * dt)
    ab_re = mag * jnp.cos(a_im * dt)
    ab_im = mag * jnp.sin(a_im * dt)
    den = a_re * a_re + a_im * a_im
    xr, yi = ab_re - 1.0, ab_im
    coef_re = (xr * a_re + yi * a_im) / den
    coef_im = (yi * a_re - xr * a_im) / den
    b_re = bre_ref[...]
    b_im = bim_ref[...]
    bbr_ref[...] = coef_re[:, None, :] * b_re - coef_im[:, None, :] * b_im
    bbi_ref[...] = coef_re[:, None, :] * b_im + coef_im[:, None, :] * b_re
    pr, pi = ab_re, ab_im
    pwr_ref[0] = pr
    pwi_ref[0] = pi
    for k in range(1, SUBLANES):
        pr, pi = pr * ab_re - pi * ab_im, pr * ab_im + pi * ab_re
        pwr_ref[k] = pr
        pwi_ref[k] = pi


def _s5_params(a_re, a_im, log_dt, b_re, b_im):
    G, P, Hc = S5_GROUPS, S5_STATE, S5_GROUP_CH
    bt_re = jnp.transpose(b_re, (0, 2, 1))
    bt_im = jnp.transpose(b_im, (0, 2, 1))
    return pl.pallas_call(
        _s5_param_kernel,
        out_shape=[jax.ShapeDtypeStruct((SUBLANES, G, P), f32), jax.ShapeDtypeStruct((SUBLANES, G, P), f32),
                   jax.ShapeDtypeStruct((G, Hc, P), f32), jax.ShapeDtypeStruct((G, Hc, P), f32)],
        name="s5_params",
    )(a_re, a_im, log_dt.reshape(G, 1), bt_re, bt_im)


def _gelu_tanh(x):
    return 0.5 * x * (1.0 + jnp.tanh(math.sqrt(2.0 / math.pi) * (x + 0.044715 * (x * x * x))))


def _s5_kernel(u_ref, wb_ref, pwr_ref, pwi_ref, wc_ref, dsk_ref, wg_ref, bg_ref, ng_ref, y_ref,
               sr_ref, si_ref, cr_ref, ci_ref):
    TT = u_ref.shape[0]
    NL = S5_LANES
    ng8 = TT // SUBLANES

    @pl.when(pl.program_id(1) == 0)
    def _():
        cr_ref[...] = jnp.zeros_like(cr_ref)
        ci_ref[...] = jnp.zeros_like(ci_ref)

    u = u_ref[...]
    bu = _dot(u.astype(bf16), wb_ref[...])
    xr = bu[:, :NL].reshape(ng8, SUBLANES, NL)
    xi = bu[:, NL:].reshape(ng8, SUBLANES, NL)
    row = lax.broadcasted_iota(i32, (1, SUBLANES, NL), 1)
    for sh in (1, 2, 4):
        ar = pwr_ref[sh - 1:sh, :][None]
        ai = pwi_ref[sh - 1:sh, :][None]
        tr = pltpu.roll(xr, sh, 1)
        ti = pltpu.roll(xi, sh, 1)
        keep = row >= sh
        xr, xi = (xr + jnp.where(keep, ar * tr - ai * ti, 0.0),
                  xi + jnp.where(keep, ar * ti + ai * tr, 0.0))
    sr_ref[...] = xr.reshape(TT, NL)
    si_ref[...] = xi.reshape(TT, NL)

    pr = pwr_ref[...]
    pi = pwi_ref[...]

    def group(gidx, carry):
        c_re, c_im = carry
        r0 = pl.multiple_of(gidx * SUBLANES, SUBLANES)
        t_re = sr_ref[pl.ds(r0, SUBLANES), :] + pr * c_re - pi * c_im
        t_im = si_ref[pl.ds(r0, SUBLANES), :] + pr * c_im + pi * c_re
        sr_ref[pl.ds(r0, SUBLANES), :] = t_re
        si_ref[pl.ds(r0, SUBLANES), :] = t_im
        return t_re[SUBLANES - 1:SUBLANES, :], t_im[SUBLANES - 1:SUBLANES, :]

    c_re, c_im = lax.fori_loop(0, ng8, group, (cr_ref[...], ci_ref[...]))
    cr_ref[...] = c_re
    ci_ref[...] = c_im

    y = (_dot(sr_ref[...].astype(bf16), wc_ref[0:NL, :]) + _dot(si_ref[...].astype(bf16), wc_ref[NL:2 * NL, :])
         + dsk_ref[...] * u)
    z = _gelu_tanh(y)
    out = z * jax.nn.sigmoid(_dot(z.astype(bf16), wg_ref[...]) + bg_ref[...])
    y_ref[...] = _rms(out, ng_ref[...]).astype(y_ref.dtype)


def _s5(su, p, batch):
    t = su.shape[0]
    s = t // batch
    tt = min(S5_BLOCK, s)
    nb = s // tt
    W, NL = S5_WIDTH, S5_LANES
    tok = lambda b, c: (b * nb + c, 0)
    return pl.pallas_call(
        _s5_kernel,
        grid=(batch, nb),
        in_specs=[pl.BlockSpec((tt, W), tok), _full((W, 2 * NL)), _full((SUBLANES, NL)), _full((SUBLANES, NL)),
                  _full((2 * NL, W)), _full((1, W)), _full((W, W)), _full((1, W)), _full((1, W))],
        out_specs=pl.BlockSpec((tt, W), tok),
        out_shape=jax.ShapeDtypeStruct((t, W), bf16),
        scratch_shapes=[pltpu.VMEM((tt, NL), f32), pltpu.VMEM((tt, NL), f32),
                        pltpu.VMEM((1, NL), f32), pltpu.VMEM((1, NL), f32)],
        compiler_params=_cparams("arbitrary", "arbitrary"),
        name="s5",
    )(su, p["wb"], p["pw_re"], p["pw_im"], p["wc"], p["d"], p["w_glu"], p["b_glu"], p["norm"])


def _t5_bucket(dist):
    exact = dist < REL_MAX_EXACT
    large = REL_MAX_EXACT + (np.log(np.maximum(dist, 1) / REL_MAX_EXACT)
                             / np.log(REL_MAX_DIST / REL_MAX_EXACT)
                             * (REL_BUCKETS - REL_MAX_EXACT)).astype(np.int32)
    large = np.minimum(large, REL_BUCKETS - 1)
    return np.where(exact, dist, large).astype(np.int32)


def _band_offsets():
    i = np.arange(ATTN_BLOCK)[:, None]
    j = np.arange(2 * ATTN_BLOCK)[None, :]
    return ATTN_BLOCK + i - j


def _attn_bias(rel_table, dil):
    bucket = _t5_bucket(np.clip(_band_offsets(), 0, None) * dil)
    pick = (bucket.reshape(-1, 1) == np.arange(REL_BUCKETS)[None, :]).astype(np.float32)
    bias = jnp.dot(pick, rel_table.astype(f32), precision=lax.Precision.HIGHEST)
    return jnp.transpose(bias.reshape(ATTN_BLOCK, 2 * ATTN_BLOCK, ATTN_HEADS), (2, 0, 1))


def _attn_kernel(q_ref, kc_ref, kp_ref, vc_ref, vp_ref, bias_ref, o_ref, lse_ref, kcat_ref, vcat_ref,
                 s_ref, p_ref, *, n_back):
    RB = q_ref.shape[0]
    BLK = ATTN_BLOCK
    n = pl.program_id(2)
    kcat_ref[0:BLK, :] = kp_ref[...]
    kcat_ref[BLK:BLK + RB, :] = kc_ref[...]
    vcat_ref[0:BLK, :] = vp_ref[...]
    vcat_ref[BLK:BLK + RB, :] = vc_ref[...]

    ri = lax.broadcasted_iota(i32, (BLK, 2 * BLK), 0)
    ci = lax.broadcasted_iota(i32, (BLK, 2 * BLK), 1)
    off = BLK + ri - ci
    band = (off >= 0) & (off <= n_back)
    lane_q = lax.broadcasted_iota(i32, (BLK, LANES), 1)
    lane_kv = lax.broadcasted_iota(i32, (2 * BLK, LANES), 1)

    upper_q = lane_q >= HEAD_DIM
    upper_kv = lane_kv >= HEAD_DIM
    pairs = ATTN_HEADS // 2

    def sub_block(j, _):
        r0 = pl.multiple_of(j * BLK, BLK)
        has_prev = (n > 0) | (j > 0)
        valid = band & (has_prev | (ci >= BLK))
        for hp in range(pairs):
            cs = slice(hp * LANES, (hp + 1) * LANES)
            qp = q_ref[pl.ds(r0, BLK), cs]
            kp = kcat_ref[pl.ds(r0, 2 * BLK), cs]
            for e in range(2):
                qm = jnp.where(upper_q == (e == 1), qp, jnp.zeros_like(qp))
                s = _dot_nt(qm, kp) + bias_ref[2 * hp + e]
                s_ref[2 * hp + e] = jnp.where(valid, s, NEG_INF)
        ms = [jnp.max(s_ref[h], axis=-1, keepdims=True) for h in range(ATTN_HEADS)]
        for h in range(ATTN_HEADS):
            p_ref[h] = jnp.exp(s_ref[h] - ms[h]).astype(bf16)
        for hp in range(pairs):
            cs = slice(hp * LANES, (hp + 1) * LANES)
            vp = vcat_ref[pl.ds(r0, 2 * BLK), cs]
            o_pair = None
            lse_pair = None
            for e in range(2):
                mine = upper_kv == (e == 1)
                ones_lane = (1 - e) * HEAD_DIM
                vm = jnp.where(mine, vp, jnp.where(lane_kv == ones_lane, 1.0, 0.0).astype(bf16))
                o_e = _dot(p_ref[2 * hp + e], vm)
                l = o_e[:, ones_lane:ones_lane + 1]
                o_e = o_e / l
                lse_e = jnp.broadcast_to(ms[2 * hp + e] + jnp.log(l), (BLK, LANES))
                o_pair = o_e if o_pair is None else jnp.where(upper_q, o_e, o_pair)
                lse_pair = lse_e if lse_pair is None else jnp.where(upper_q, lse_e, lse_pair)
            o_ref[pl.ds(r0, BLK), cs] = o_pair
            lse_ref[pl.ds(r0, BLK), cs] = lse_pair
        return 0

    lax.fori_loop(0, RB // BLK, sub_block, 0)


def _dilated_attn(q, k, v, bias, window, dil, batch):
    w = ATTN_WIDTH
    n = q.shape[0] // batch
    rb = min(TOKEN_BLOCK, n)
    nblk = n // rb
    per = rb // ATTN_BLOCK
    view = lambda a: a.reshape(batch, n, dil * w)
    cur = pl.BlockSpec((None, rb, w), lambda b, r, i: (b, i, r))
    prev = pl.BlockSpec((None, ATTN_BLOCK, w), lambda b, r, i: (b, jnp.maximum(i * per - 1, 0), r))
    out = pl.BlockSpec((None, rb, w), lambda b, r, i: (b, i, r))
    o, lse = pl.pallas_call(
        functools.partial(_attn_kernel, n_back=window // dil),
        grid=(batch, dil, nblk),
        in_specs=[cur, cur, prev, cur, prev, _full(bias.shape)],
        out_specs=[out, out],
        out_shape=[jax.ShapeDtypeStruct((batch, n, dil * w), f32)] * 2,
        scratch_shapes=[pltpu.VMEM((rb + ATTN_BLOCK, w), bf16), pltpu.VMEM((rb + ATTN_BLOCK, w), bf16),
                        pltpu.VMEM((ATTN_HEADS, ATTN_BLOCK, 2 * ATTN_BLOCK), f32),
                        pltpu.VMEM((ATTN_HEADS, ATTN_BLOCK, 2 * ATTN_BLOCK), bf16)],
        compiler_params=_cparams("parallel", "parallel", "arbitrary"),
        name=f"dilated_attn_{dil}",
    )(view(q), view(k), view(k), view(v), view(v), bias)
    return o.reshape(batch * n, dil * w), lse.reshape(batch * n, dil * w)


def _token_order(ref, scr_ref, slot, dil, tm):
    if dil == 1:
        return ref[...]
    for r in range(dil):
        for c in range(ATTN_TILES):
            col = r * ATTN_WIDTH + c * LANES
            scr_ref[slot, c, pl.ds(r, tm // dil, stride=dil), :] = ref[:, col:col + LANES]
    return jnp.concatenate([scr_ref[slot, c] for c in range(ATTN_TILES)], axis=1)


def _out_proj_kernel(x_ref, ym_ref, ys_ref, o1_ref, o2_ref, o3_ref, l1_ref, l2_ref, l3_ref, ag_ref,
                     wm_ref, ws_ref, wa_ref, out_ref, scr_ref):
    tm = x_ref.shape[0]
    o_refs, l_refs = (o1_ref, o2_ref, o3_ref), (l1_ref, l2_ref, l3_ref)
    o = [_token_order(r, scr_ref, 2 * i, dil, tm) for i, (r, dil) in enumerate(zip(o_refs, DILATIONS))]
    lse = [_token_order(r, scr_ref, 2 * i + 1, dil, tm) for i, (r, dil) in enumerate(zip(l_refs, DILATIONS))]
    mx = jnp.maximum(jnp.maximum(lse[0], lse[1]), lse[2])
    e = [jnp.exp(l - mx) for l in lse]
    tot = e[0] + e[1] + e[2]
    ya = (e[0] / tot) * o[0] + (e[1] / tot) * o[1] + (e[2] / tot) * o[2]
    ya = _rms(ya, ag_ref[...]).astype(bf16)
    out_ref[...] = (x_ref[...] + _dot(ym_ref[...], wm_ref[...]) + _dot(ys_ref[...], ws_ref[...])
                    + _dot(ya, wa_ref[...]))


def _out_proj(x2, ym, ys, attn, a_gain, wm, ws, wa):
    t, d = x2.shape
    tm = min(TOKEN_BLOCK, t)
    row = lambda w: pl.BlockSpec((tm, w), lambda i: (i, 0))
    dilated = [pl.BlockSpec((tm // dil, dil * ATTN_WIDTH), lambda i: (i, 0)) for dil in DILATIONS]
    (o1, l1), (o2, l2), (o3, l3) = attn
    return pl.pallas_call(
        _out_proj_kernel,
        grid=(t // tm,),
        in_specs=[row(d), row(MLSTM_PAD), row(S5_WIDTH)] + dilated + dilated
                 + [_full((1, ATTN_WIDTH)), _full(wm.shape), _full(ws.shape), _full(wa.shape)],
        out_specs=row(d),
        out_shape=jax.ShapeDtypeStruct((t, d), f32),
        scratch_shapes=[pltpu.VMEM((2 * len(DILATIONS), ATTN_TILES, tm, LANES), f32)],
        compiler_params=_cparams("parallel"),
        name="out_proj",
    )(x2, ym, ys, o1, o2, o3, l1, l2, l3, a_gain, wm, ws, wa)


def _mem_kv_kernel(mem_ref, g_ref, w_ref, kv_ref):
    kv_ref[...] = _dot(_rms(mem_ref[...], g_ref[...]).astype(bf16), w_ref[...]).astype(kv_ref.dtype)


def _mem_kv(mem2, gain, w_kv):
    r, d = mem2.shape
    return pl.pallas_call(
        _mem_kv_kernel,
        out_shape=jax.ShapeDtypeStruct((r, 2 * d), bf16),
        compiler_params=pltpu.CompilerParams(vmem_limit_bytes=VMEM_LIMIT),
        name="mem_kv",
    )(mem2, gain, w_kv)


def _xattn_kernel(x_ref, g_ref, wq_ref, kv_ref, wo_ref, out_ref):
    x = x_ref[...]
    d = x.shape[1]
    hd = d // MEM_HEADS
    q = _dot(_rms(x, g_ref[...]).astype(bf16), wq_ref[...]).astype(bf16)
    heads = []
    for h in range(MEM_HEADS):
        kh = kv_ref[:, h * hd:(h + 1) * hd]
        vh = kv_ref[:, d + h * hd:d + (h + 1) * hd]
        s = _dot_nt(q[:, h * hd:(h + 1) * hd], kh) * (hd ** -0.5)
        m = jnp.max(s, axis=-1, keepdims=True)
        p = jnp.exp(s - m)
        p = p / jnp.sum(p, axis=-1, keepdims=True)
        heads.append(_dot(p.astype(bf16), vh).astype(bf16))
    o = jnp.concatenate(heads, axis=-1)
    out_ref[...] = x + _dot(o, wo_ref[...])


def _xattn(x2, gain, w_q, kv, w_o, batch):
    t, d = x2.shape
    s = t // batch
    m = kv.shape[0] // batch
    tm = min(TOKEN_BLOCK, s)
    nb = s // tm
    tok = lambda b, i: (b * nb + i, 0)
    return pl.pallas_call(
        _xattn_kernel,
        grid=(batch, nb),
        in_specs=[pl.BlockSpec((tm, d), tok), _full((1, d)), _full((d, d)),
                  pl.BlockSpec((m, 2 * d), lambda b, i: (b, 0)), _full((d, d))],
        out_specs=pl.BlockSpec((tm, d), tok),
        out_shape=jax.ShapeDtypeStruct((t, d), f32),
        compiler_params=_cparams("parallel", "parallel"),
        name="xattn",
    )(x2, gain, w_q, kv, w_o)


MOE_UNIT = SUBLANES
MOE_BLOCK_UNITS = MOE_ROWS // MOE_UNIT
GATE_PIECES = 3


def _moe_slots(tm):
    worst = 2 * tm + N_EXPERTS * (MOE_UNIT - 1)
    return -(-worst // LANES) * LANES


def _first_argmax(x, lane):
    m = jnp.max(x, axis=-1, keepdims=True)
    idx = jnp.min(jnp.where(x == m, lane, LANES), axis=-1, keepdims=True)
    return m, idx


def _router_kernel(x_ref, g_ref, wr_ref, br_ref, xn_ref, sel_ref, gate_ref, cnt_ref):
    TM = x_ref.shape[0]
    G, EPG, E = MOE_GROUPS, EXPERTS_PER_GROUP, N_EXPERTS
    xn = _rms(x_ref[...], g_ref[...])
    xn_ref[...] = xn.astype(xn_ref.dtype)
    logits = _dot_exact(xn, wr_ref[...]) + br_ref[...]
    lane = lax.broadcasted_iota(i32, (TM, LANES), 1)
    is_group = (lane >= E) & (lane < E + G)
    gl = jnp.where(is_group, logits, -jnp.inf)
    gmax, gidx = _first_argmax(gl, lane)
    g_p = 1.0 / jnp.sum(jnp.exp(gl - gmax), axis=-1, keepdims=True)
    g_i = gidx - E
    in_group = (lane < E) & ((lane >> int(math.log2(EPG))) == g_i)
    el = jnp.where(in_group, logits, -jnp.inf)
    m1, i1 = _first_argmax(el, lane)
    m2, i2 = _first_argmax(jnp.where(lane == i1, -jnp.inf, el), lane)
    p2 = jnp.exp(m2 - m1)
    gate1 = g_p / (1.0 + p2)
    gate2 = g_p * p2 / (1.0 + p2)

    onehot = ((lane == i1) | (lane == i2)).astype(bf16)
    r = lax.broadcasted_iota(i32, (TM, TM), 0)
    c = lax.broadcasted_iota(i32, (TM, TM), 1)
    before = _dot((c < r).astype(bf16), onehot)
    cnt = jnp.sum(onehot.astype(f32), axis=0, keepdims=True)
    padded = jnp.floor((cnt + (MOE_UNIT - 1)) * (1.0 / MOE_UNIT)) * MOE_UNIT
    er = lax.broadcasted_iota(i32, (LANES, LANES), 0)
    ec = lax.broadcasted_iota(i32, (LANES, LANES), 1)
    padded8 = jnp.broadcast_to(padded, (SUBLANES, LANES))
    seg_start = _dot_exact(padded8, (er < ec).astype(f32))[0:1, :]
    slot = seg_start + before
    pos1 = jnp.sum(jnp.where(lane == i1, slot, 0.0), axis=-1, keepdims=True)
    pos2 = jnp.sum(jnp.where(lane == i2, slot, 0.0), axis=-1, keepdims=True)
    sel_ref[...] = jnp.where(lane == 0, pos1, jnp.where(lane == 1, pos2, 0.0))
    gate_ref[...] = jnp.where(lane == 0, gate1, jnp.where(lane == 1, gate2, 0.0))
    cnt_ref[...] = padded8.astype(i32)


def _router(x2, gain, w_r, b_r):
    t, d = x2.shape
    tm = min(TOKEN_BLOCK, t)
    nblk = t // tm
    row = lambda w: pl.BlockSpec((tm, w), lambda i: (i, 0))
    return pl.pallas_call(
        _router_kernel,
        grid=(nblk,),
        in_specs=[row(d), _full((1, d)), _full((d, LANES)), _full((1, LANES))],
        out_specs=[row(d), row(LANES), row(LANES), pl.BlockSpec((None, SUBLANES, LANES), lambda i: (i, 0, 0))],
        out_shape=[jax.ShapeDtypeStruct((t, d), bf16), jax.ShapeDtypeStruct((t, LANES), f32),
                   jax.ShapeDtypeStruct((t, LANES), f32), jax.ShapeDtypeStruct((nblk, SUBLANES, LANES), i32)],
        compiler_params=_cparams("parallel"),
        name="moe_router",
    )(x2, gain, w_r, b_r)


def _unit_copy(src_ref, src_unit, dst_ref, dst_unit, sem):
    src = src_ref.at[pl.ds(pl.multiple_of(src_unit * MOE_UNIT, MOE_UNIT), MOE_UNIT), :]
    dst = dst_ref.at[pl.ds(pl.multiple_of(dst_unit * MOE_UNIT, MOE_UNIT), MOE_UNIT), :]
    return pltpu.make_async_copy(src, dst, sem)


def _wait_units(n, src_ref, dst_ref, sem):
    p = 1 << (max(src_ref.shape[0], dst_ref.shape[0]) // MOE_UNIT).bit_length()
    while p > 1:
        p //= 2
        rows = p * MOE_UNIT
        if rows > min(src_ref.shape[0], dst_ref.shape[0]):
            continue

        @pl.when((n & p) != 0)
        def _():
            pltpu.make_async_copy(src_ref.at[pl.ds(0, rows), :], dst_ref.at[pl.ds(0, rows), :], sem).wait()


def _gate_pieces(g, lane):
    hi = g.astype(bf16).astype(f32)
    r1 = g - hi
    mid = r1.astype(bf16).astype(f32)
    lo = r1 - mid
    return jnp.where(lane == 0, hi, jnp.where(lane == 1, mid, jnp.where(lane == 2, lo, 0.0))).astype(bf16)


def _dispatch_kernel(unit_ref, nv_ref, tail_ref, xn_ref, sel_ref, gate_ref, rows_ref, sorted_ref, zero_ref, sems):
    TM, D = xn_ref.shape
    NS = sorted_ref.shape[1]
    NU = NS // MOE_UNIT
    tb = pl.program_id(0)
    sem = sems.at[0]

    @pl.when(tb == 0)
    def _():
        zero_ref[...] = jnp.zeros_like(zero_ref)

        def zero_copy(e):
            start = pl.multiple_of(tail_ref[e] * MOE_UNIT - MOE_ROWS, MOE_UNIT)
            return pltpu.make_async_copy(zero_ref, rows_ref.at[pl.ds(start, MOE_ROWS), :], sem)

        def issue(e, _):
            @pl.when(tail_ref[N_EXPERTS + e] > 0)
            def _():
                zero_copy(e).start()
            return 0

        def drain(e, _):
            @pl.when(tail_ref[N_EXPERTS + e] > 0)
            def _():
                zero_copy(e).wait()
            return 0

        lax.fori_loop(0, N_EXPERTS, issue, 0)
        lax.fori_loop(0, N_EXPERTS, drain, 0)

        def block_copy(i):
            start = pl.multiple_of(i * MOE_ROWS, MOE_ROWS)
            return pltpu.make_async_copy(zero_ref, rows_ref.at[pl.ds(start, MOE_ROWS), :], sem)

        def issue_block(i, _):
            block_copy(i).start()
            return 0

        def drain_block(i, _):
            block_copy(i).wait()
            return 0

        n_used = tail_ref[2 * N_EXPERTS]
        lax.fori_loop(n_used, rows_ref.shape[0] // MOE_ROWS, issue_block, 0)
        lax.fori_loop(n_used, rows_ref.shape[0] // MOE_ROWS, drain_block, 0)

    sel_t = jnp.transpose(sel_ref[...])
    slot = lax.broadcasted_iota(i32, (NS, TM), 0).astype(f32)
    p1 = jnp.where(slot == sel_t[0:1, :], 1.0, 0.0).astype(bf16)
    p2 = jnp.where(slot == sel_t[1:2, :], 1.0, 0.0).astype(bf16)
    lane = lax.broadcasted_iota(i32, (TM, LANES), 1)
    gates = gate_ref[...]
    cur = lax.rem(tb, 2)
    sorted_ref[cur, :, 0:D] = _dot(p1 + p2, xn_ref[...])
    sorted_ref[cur, :, D:D + LANES] = (_dot(p1, _gate_pieces(gates[:, 0:1], lane))
                                       + _dot(p2, _gate_pieces(gates[:, 1:2], lane)))

    @pl.when(tb > 0)
    def _():
        _wait_units(nv_ref[tb - 1], sorted_ref.at[1 - cur], rows_ref, sems.at[1 - cur])

    def start(j, _):
        _unit_copy(sorted_ref.at[cur], j, rows_ref, unit_ref[tb * NU + j], sems.at[cur]).start()
        return 0

    lax.fori_loop(0, nv_ref[tb], start, 0)

    @pl.when(tb == pl.num_programs(0) - 1)
    def _():
        _wait_units(nv_ref[tb], sorted_ref.at[cur], rows_ref, sems.at[cur])


def _dispatch(unit_map, n_valid, tail, xn, sel, gates, n_rows):
    t, d = xn.shape
    tm = min(TOKEN_BLOCK, t)
    ns = _moe_slots(tm)
    row = lambda w: pl.BlockSpec((tm, w), lambda i, um, nv, tl: (i, 0))
    return pl.pallas_call(
        _dispatch_kernel,
        grid_spec=pltpu.PrefetchScalarGridSpec(
            num_scalar_prefetch=3,
            grid=(t // tm,),
            in_specs=[row(d), row(LANES), row(LANES)],
            out_specs=pl.BlockSpec(memory_space=pl.ANY),
            scratch_shapes=[pltpu.VMEM((2, ns, d + LANES), f32), pltpu.VMEM((MOE_ROWS, d + LANES), f32),
                            pltpu.SemaphoreType.DMA((2,))],
        ),
        out_shape=jax.ShapeDtypeStruct((n_rows, d + LANES), f32),
        compiler_params=_cparams("arbitrary"),
        name="moe_dispatch",
    )(unit_map, n_valid, tail, xn, sel, gates)


def _expert_kernel(be_ref, nu_ref, x_ref, wgu_ref, wd_ref, y_ref, wgu_bf, wd_bf):
    i = pl.program_id(0)
    de, d = wd_ref.shape

    @pl.when((i == 0) | (be_ref[i] != be_ref[jnp.maximum(i - 1, 0)]))
    def _():
        wgu_bf[...] = wgu_ref[...].astype(bf16)
        wd_bf[...] = wd_ref[...].astype(bf16)

    @pl.when(i < nu_ref[0])
    def _():
        gate = x_ref[:, d:d + 1]
        for k in range(1, GATE_PIECES):
            gate = gate + x_ref[:, d + k:d + k + 1]
        ab = _dot(x_ref[:, 0:d].astype(bf16), wgu_bf[...])
        a, b = ab[:, :de], ab[:, de:]
        hid = (a * jax.nn.sigmoid(a) * b).astype(bf16)
        y_ref[...] = _dot(hid, wd_bf[...]) * gate

    @pl.when(i >= nu_ref[0])
    def _():
        y_ref[...] = jnp.zeros_like(y_ref)


def _experts(block_e, n_used, rows, w_gu, w_down, layer):
    n_rows, width = rows.shape
    nblk = n_rows // MOE_ROWS
    de, d = w_down.shape[2:]
    used = lambda i, nu: jnp.maximum(jnp.minimum(i, nu[0] - 1), 0)
    return pl.pallas_call(
        _expert_kernel,
        grid_spec=pltpu.PrefetchScalarGridSpec(
            num_scalar_prefetch=2,
            grid=(nblk,),
            in_specs=[pl.BlockSpec((MOE_ROWS, width), lambda i, be, nu: (used(i, nu), 0)),
                      pl.BlockSpec((None, None, d, 2 * de), lambda i, be, nu: (layer, be[i], 0, 0)),
                      pl.BlockSpec((None, None, de, d), lambda i, be, nu: (layer, be[i], 0, 0))],
            out_specs=pl.BlockSpec((MOE_ROWS, d), lambda i, be, nu: (i, 0)),
            scratch_shapes=[pltpu.VMEM((d, 2 * de), bf16), pltpu.VMEM((de, d), bf16)],
        ),
        out_shape=jax.ShapeDtypeStruct((n_rows, d), f32),
        compiler_params=_cparams("arbitrary"),
        name="moe_experts",
    )(block_e, n_used, rows, w_gu, w_down)


def _combine_kernel(unit_ref, nv_ref, x_ref, sel_ref, fg_ref, y_hbm, out_ref, ybuf, sems, *, final_norm):
    TM = x_ref.shape[0]
    NS, D = ybuf.shape[1:]
    NU = NS // MOE_UNIT
    tb = pl.program_id(0)
    cur = lax.rem(tb, 2)

    def fetch(block, buf):
        def start(j, _):
            _unit_copy(y_hbm, unit_ref[block * NU + j], ybuf.at[buf], j, sems.at[buf]).start()
            return 0

        def clear(j, _):
            ybuf[buf, pl.ds(pl.multiple_of(j * MOE_UNIT, MOE_UNIT), MOE_UNIT), :] = jnp.zeros((MOE_UNIT, D), f32)
            return 0

        lax.fori_loop(0, nv_ref[block], start, 0)
        lax.fori_loop(nv_ref[block], NU, clear, 0)

    @pl.when(tb == 0)
    def _():
        fetch(tb, cur)

    @pl.when(tb + 1 < pl.num_programs(0))
    def _():
        fetch(tb + 1, 1 - cur)

    _wait_units(nv_ref[tb], y_hbm, ybuf.at[cur], sems.at[cur])

    sel = sel_ref[...]
    slot = lax.broadcasted_iota(i32, (TM, NS), 1).astype(f32)
    pick = jnp.where((slot == sel[:, 0:1]) | (slot == sel[:, 1:2]), 1.0, 0.0).astype(bf16)
    y = ybuf[cur]
    hi = y.astype(bf16)
    lo = (y - hi.astype(f32)).astype(bf16)
    out = x_ref[...] + _dot(pick, hi) + _dot(pick, lo)
    if final_norm:
        out = _rms(out, fg_ref[...])
    out_ref[...] = out


def _combine(unit_map, n_valid, x2, sel, y_rows, final_gain, final_norm):
    t, d = x2.shape
    tm = min(TOKEN_BLOCK, t)
    ns = _moe_slots(tm)
    return pl.pallas_call(
        functools.partial(_combine_kernel, final_norm=final_norm),
        grid_spec=pltpu.PrefetchScalarGridSpec(
            num_scalar_prefetch=2,
            grid=(t // tm,),
            in_specs=[pl.BlockSpec((tm, d), lambda i, um, nv: (i, 0)),
                      pl.BlockSpec((tm, LANES), lambda i, um, nv: (i, 0)),
                      pl.BlockSpec((1, d), lambda i, um, nv: (0, 0)),
                      pl.BlockSpec(memory_space=pl.ANY)],
            out_specs=pl.BlockSpec((tm, d), lambda i, um, nv: (i, 0)),
            scratch_shapes=[pltpu.VMEM((2, ns, d), f32), pltpu.SemaphoreType.DMA((2,))],
        ),
        out_shape=jax.ShapeDtypeStruct((t, d), f32),
        compiler_params=_cparams("arbitrary"),
        name="moe_combine",
    )(unit_map, n_valid, x2, sel, final_gain, y_rows)


def _moe(x2, gain, w_r, b_r, w_gu, w_down, layer, final_gain, final_norm):
    t, d = x2.shape
    tm = min(TOKEN_BLOCK, t)
    nblk = t // tm
    nu = _moe_slots(tm) // MOE_UNIT
    xn, sel, gates, cnt = _router(x2, gain, w_r, b_r)
    units = cnt[:, 0, :N_EXPERTS] // MOE_UNIT
    seg_end = jnp.cumsum(units, axis=1)
    seg_start = seg_end - units
    run_off = jnp.cumsum(units, axis=0) - units
    total = jnp.sum(units, axis=0)
    region = (total + MOE_BLOCK_UNITS - 1) // MOE_BLOCK_UNITS * MOE_BLOCK_UNITS
    g_end = jnp.cumsum(region)
    g_start = g_end - region
    j = jnp.arange(nu, dtype=i32)
    owner = jnp.sum((seg_end[:, None, :] <= j[None, :, None]).astype(i32), axis=2)
    hit = owner[:, :, None] == jnp.arange(N_EXPERTS, dtype=i32)[None, None, :]
    shift = (g_start[None, :] + run_off - seg_start)[:, None, :]
    unit_map = jnp.where(owner < N_EXPERTS, j[None, :] + jnp.sum(jnp.where(hit, shift, 0), axis=2), 0)
    unit_map = unit_map.reshape(-1).astype(i32)
    n_valid = seg_end[:, -1].astype(i32)
    tail = jnp.concatenate([g_end, region - total, g_end[-1:] // MOE_BLOCK_UNITS]).astype(i32)
    n_blocks = -(-(nblk * nu * MOE_UNIT + N_EXPERTS * (MOE_ROWS - MOE_UNIT)) // MOE_ROWS)
    block_start = jnp.arange(n_blocks, dtype=i32) * MOE_BLOCK_UNITS
    block_e = jnp.minimum(jnp.sum((g_end[None, :] <= block_start[:, None]).astype(i32), axis=1), N_EXPERTS - 1)
    n_used = (g_end[-1:] // MOE_BLOCK_UNITS).astype(i32)
    rows = _dispatch(unit_map, n_valid, tail, xn, sel, gates, n_blocks * MOE_ROWS)
    y_rows = _experts(block_e, n_used, rows, w_gu, w_down, layer)
    return _combine(unit_map, n_valid, x2, sel, y_rows, final_gain, final_norm)


def _layer_params(l, a):
    H, W = MLSTM_HEADS, MLSTM_WIDTH
    w_in = a["w_in"][l]
    sizes = [W, W, H, H, S5_WIDTH, ATTN_WIDTH, ATTN_WIDTH, ATTN_WIDTH]
    offs = np.concatenate([[0], np.cumsum(sizes)])
    seg = lambda i: w_in[:, offs[i]:offs[i + 1]]
    gates_w = jnp.pad(jnp.concatenate([seg(2), seg(3)], axis=1), ((0, 0), (0, LANES - 2 * H)))
    w_cat = jnp.concatenate([_pad_heads(seg(0), 1), _pad_heads(seg(1), 1), gates_w, seg(4),
                             seg(5) * (HEAD_DIM ** -0.5), seg(6), seg(7)], axis=1).astype(bf16)
    row = lambda v: v.reshape(1, -1)
    gb = a["mlstm_gate_bias"][l]
    mlstm = dict(
        conv_w=_pad_heads(a["mlstm_conv_w"][l], 1), conv_b=row(_pad_heads(a["mlstm_conv_b"][l], 0)),
        wq=_block_diag_heads(a["mlstm_wq"][l] * (HEAD_DIM ** -0.5)).astype(bf16),
        wk=_block_diag_heads(a["mlstm_wk"][l]).astype(bf16),
        wv=_block_diag_heads(a["mlstm_wv"][l]).astype(bf16),
        gb_row=jnp.pad(gb, (0, LANES - 2 * H)).reshape(1, LANES), gb_col=gb.reshape(2 * H, 1),
        norm=row(_pad_heads(a["mlstm_norm"][l], 0)), skip=row(_pad_heads(a["mlstm_skip"][l], 0)))

    G, P, Hc = S5_GROUPS, S5_STATE, S5_GROUP_CH
    pw_re, pw_im, bb_re, bb_im = _s5_params(a["s5_a_re"][l], a["s5_a_im"][l], a["s5_log_dt"][l],
                                            a["s5_b_re"][l], a["s5_b_im"][l])
    eye = jnp.eye(G, dtype=f32)
    bd_in = lambda m: (m[:, :, None, :] * eye[:, None, :, None]).reshape(G * Hc, G * P)
    bd_out = lambda m: (jnp.transpose(m, (0, 2, 1))[:, :, None, :] * eye[:, None, :, None]).reshape(G * P, G * Hc)
    s5 = dict(
        wb=jnp.concatenate([bd_in(bb_re), bd_in(bb_im)], axis=1).astype(bf16),
        wc=jnp.concatenate([bd_out(a["s5_c_re"][l]), -bd_out(a["s5_c_im"][l])], axis=0).astype(bf16),
        pw_re=pw_re.reshape(SUBLANES, G * P), pw_im=pw_im.reshape(SUBLANES, G * P),
        d=a["s5_d"][l].reshape(1, S5_WIDTH), w_glu=a["s5_w_glu"][l].astype(bf16),
        b_glu=row(a["s5_b_glu"][l]), norm=row(a["s5_out_norm"][l]))

    w_out = a["w_out"][l]
    w_r = jnp.pad(jnp.concatenate([a["router_w_expert"][l], a["router_w_group"][l]], axis=1),
                  ((0, 0), (0, LANES - N_EXPERTS - MOE_GROUPS)))
    b_r = jnp.pad(jnp.concatenate([a["router_b_expert"][l], a["router_b_group"][l]]),
                  (0, LANES - N_EXPERTS - MOE_GROUPS)).reshape(1, LANES)
    return dict(
        norm_mix=row(a["norm_mix"][l]), w_cat=w_cat, mlstm=mlstm, s5=s5,
        attn_norm=row(a["attn_out_norm"][l]),
        wo_m=_pad_heads(w_out[:W], 0).astype(bf16), wo_s=w_out[W:W + S5_WIDTH].astype(bf16),
        wo_a=w_out[W + S5_WIDTH:].astype(bf16),
        norm_xattn=row(a["norm_xattn"][l]), norm_mem=row(a["norm_mem"][l]),
        xq=a["xattn_w_q"][l].astype(bf16), xkv=a["xattn_w_kv"][l].astype(bf16), xo=a["xattn_w_o"][l].astype(bf16),
        norm_ffn=row(a["norm_ffn"][l]), w_r=w_r, b_r=b_r,
        w_gu=a["expert_w_gu"], w_down=a["expert_w_down"])


def _forward(a):
    x = a["x"]
    batch, seq, d = x.shape
    depth = a["w_in"].shape[0]
    x2 = x.reshape(batch * seq, d)
    mem2 = a["mem"].reshape(-1, d)
    biases = [_attn_bias(a["rel_bias"], dil) for _, dil in DILATED_PATTERNS]
    final_gain = a["final_norm"].reshape(1, d)
    for l in range(depth):
        p = _layer_params(l, a)
        mu, mo, gates, su, *qkv = _in_proj(x2, p["norm_mix"], p["w_cat"])
        ym = _mlstm(mu, mo, gates, p["mlstm"], batch)
        ys = _s5(su, p["s5"], batch)
        nd = len(DILATIONS)
        attn = [_dilated_attn(qkv[i], qkv[nd + i], qkv[2 * nd + i], bias, window, dil, batch)
                for i, (bias, (window, dil)) in enumerate(zip(biases, DILATED_PATTERNS))]
        x2 = _out_proj(x2, ym, ys, attn, p["attn_norm"], p["wo_m"], p["wo_s"], p["wo_a"])
        kv = _mem_kv(mem2, p["norm_mem"], p["xkv"])
        x2 = _xattn(x2, p["norm_xattn"], p["xq"], kv, p["xo"], batch)
        x2 = _moe(x2, p["norm_ffn"], p["w_r"], p["b_r"], p["w_gu"], p["w_down"], l, final_gain,
                  final_norm=(l == depth - 1))
    return x2.reshape(batch, seq, d)


def kernel(x, mem, rel_bias, norm_mix, w_in, mlstm_conv_w, mlstm_conv_b, mlstm_wq, mlstm_wk, mlstm_wv,
           mlstm_gate_bias, mlstm_norm, mlstm_skip, s5_a_re, s5_a_im, s5_log_dt, s5_b_re, s5_b_im, s5_c_re,
           s5_c_im, s5_d, s5_w_glu, s5_b_glu, s5_out_norm, attn_out_norm, w_out, norm_xattn, norm_mem,
           xattn_w_q, xattn_w_kv, xattn_w_o, norm_ffn, router_w_group, router_b_group, router_w_expert,
           router_b_expert, expert_w_gu, expert_w_down, final_norm):
    return _forward(dict(locals()))
```
